```python
import jax, jax.numpy as jnp
from jax import lax
import numpy as np

D_MODEL = 1024
BATCH = 8
SEQ = 2048
DEPTH = 4

CHUNK = 64
N_MEM = 256
N_A_LAYERS = DEPTH // 2
N_B_LAYERS = DEPTH - N_A_LAYERS
MAIN_WIDTH = 3 * D_MODEL // 4
MEM_WIDTH = D_MODEL // 4
MIX_WIDTH = MAIN_WIDTH + MEM_WIDTH
HG_HEAD_DIM = 128
HG_HEADS = MAIN_WIDTH // HG_HEAD_DIM
FOX_HEAD_DIM = 64
FOX_HEADS = MAIN_WIDTH // FOX_HEAD_DIM
MEM_HEADS = 4
MEM_HEAD_DIM = MEM_WIDTH // MEM_HEADS
D_FF = 2816
Q_BLOCK = 128
EPS = 1e-6
A_IN_WIDTH = 4 * MAIN_WIDTH + MEM_WIDTH
B_IN_WIDTH = 2 * MAIN_WIDTH + MEM_WIDTH
KV_WIDTH = 2 * MAIN_WIDTH + FOX_HEADS

kernel_name = 'hybrid_hgrn2_fox_yoco_macaron'


def rms_norm(x, gain):
    x32 = x.astype(jnp.float32)
    y = x32 * lax.rsqrt(jnp.mean(x32 * x32, axis=-1, keepdims=True) + EPS)
    return (y * gain.astype(jnp.float32)).astype(x.dtype)


def swiglu(h, w_gate, w_up, w_down):
    return (jax.nn.silu(h @ w_gate) * (h @ w_up)) @ w_down


def split_heads(t, n_heads):
    b, s, _ = t.shape
    return t.reshape(b, s, n_heads, -1).transpose(0, 2, 1, 3)


def merge_heads(t):
    b, h, s, d = t.shape
    return t.transpose(0, 2, 1, 3).reshape(b, s, h * d)


def hgrn2_recurrence(q, k, v, log_f):
    b, h, s, dk = q.shape
    dv = v.shape[-1]
    n = s // CHUNK

    def to_chunks(t):
        return t.reshape(b, h, n, CHUNK, t.shape[-1]).transpose(2, 0, 1, 3, 4)

    qc, kc, vc = to_chunks(q), to_chunks(k), to_chunks(v)
    cum = jnp.cumsum(to_chunks(log_f), axis=-2)
    tri = jnp.tril(jnp.ones((CHUNK, CHUNK), dtype=bool))

    def step(state, inp):
        q_, k_, v_, c_ = inp
        diff = c_[:, :, :, None, :] - c_[:, :, None, :, :]
        decay = jnp.exp(jnp.where(tri[:, :, None], diff, -jnp.inf))
        scores = jnp.einsum('bhtd,bhsd,bhtsd->bhts', q_, k_, decay)
        o = (jnp.einsum('bhts,bhsv->bhtv', scores, v_)
             + jnp.einsum('bhtd,bhdv->bhtv', q_ * jnp.exp(c_), state))
        c_end = c_[:, :, -1, :]
        state = (jnp.exp(c_end)[..., None] * state
                 + jnp.einsum('bhsd,bhsv->bhdv', k_ * jnp.exp(c_end[:, :, None, :] - c_), v_))
        return state, o

    state0 = jnp.zeros((b, h, dk, dv), jnp.float32)
    _, o = lax.scan(step, state0, (qc, kc, vc, cum))
    return o.transpose(1, 2, 0, 3, 4).reshape(b, h, s, dv)


def forgetting_attention(q, k, v, cum_log_f):
    s = q.shape[2]
    scale = FOX_HEAD_DIM ** -0.5
    outs = []
    for blk in range(s // Q_BLOCK):
        start = blk * Q_BLOCK
        end = start + Q_BLOCK
        logits = jnp.einsum('bhqd,bhkd->bhqk', q[:, :, start:end], k[:, :, :end]).astype(jnp.float32) * scale
        logits = logits + cum_log_f[:, :, start:end, None] - cum_log_f[:, :, None, :end]
        causal = jnp.arange(end)[None, :] <= jnp.arange(start, end)[:, None]
        p = jax.nn.softmax(jnp.where(causal, logits, -jnp.inf), axis=-1)
        outs.append(jnp.einsum('bhqk,bhkd->bhqd', p.astype(v.dtype), v[:, :, :end]))
    return jnp.concatenate(outs, axis=2)


def memory_attention(qm_raw, mem_n, w_mem_kv, q_gain, k_gain):
    kv = mem_n @ w_mem_kv
    km = rms_norm(split_heads(kv[..., :MEM_WIDTH], MEM_HEADS), k_gain)
    vm = split_heads(kv[..., MEM_WIDTH:], MEM_HEADS)
    qm = rms_norm(split_heads(qm_raw, MEM_HEADS), q_gain)
    logits = jnp.einsum('bhqd,bhkd->bhqk', qm, km).astype(jnp.float32) * (MEM_HEAD_DIM ** -0.5)
    p = jax.nn.softmax(logits, axis=-1)
    return merge_heads(jnp.einsum('bhqk,bhkd->bhqd', p.astype(vm.dtype), vm))


def setup_inputs(seed: int = 0) -> dict:
    key = jax.random.key(seed)
    ks = jax.random.split(key, 32)

    def w(k, shape, fan_in):
        return jax.random.normal(k, shape, jnp.float32) * (fan_in ** -0.5)

    def g(k, shape):
        return 1.0 + 0.02 * jax.random.normal(k, shape, jnp.float32)

    return {
        'x': jax.random.normal(ks[0], (BATCH, SEQ, D_MODEL), jnp.float32),
        'mem': jax.random.normal(ks[1], (BATCH, N_MEM, D_MODEL), jnp.float32),
        'ffn1_norm': g(ks[2], (DEPTH, D_MODEL)),
        'ffn1_w_gate': w(ks[3], (DEPTH, D_MODEL, D_FF), D_MODEL),
        'ffn1_w_up': w(ks[4], (DEPTH, D_MODEL, D_FF), D_MODEL),
        'ffn1_w_down': w(ks[5], (DEPTH, D_FF, D_MODEL), D_FF),
        'mix_norm': g(ks[6], (DEPTH, D_MODEL)),
        'mem_norm': g(ks[7], (DEPTH, D_MODEL)),
        'w_mem_kv': w(ks[8], (DEPTH, D_MODEL, 2 * MEM_WIDTH), D_MODEL),
        'mem_q_gain': g(ks[9], (DEPTH, MEM_HEAD_DIM)),
        'mem_k_gain': g(ks[10], (DEPTH, MEM_HEAD_DIM)),
        'w_in_a': w(ks[11], (N_A_LAYERS, D_MODEL, A_IN_WIDTH), D_MODEL),
        'hgrn_lb_logits': jax.random.normal(ks[12], (N_A_LAYERS, MAIN_WIDTH), jnp.float32),
        'hgrn_o_gain': g(ks[13], (N_A_LAYERS, HG_HEAD_DIM)),
        'w_in_b': w(ks[14], (N_B_LAYERS, D_MODEL, B_IN_WIDTH), D_MODEL),
        'fox_q_gain': g(ks[15], (N_B_LAYERS, FOX_HEAD_DIM)),
        'kv_norm': g(ks[16], (D_MODEL,)),
        'w_kv': w(ks[17], (D_MODEL, KV_WIDTH), D_MODEL),
        'fox_f_bias': 0.1 * jax.random.normal(ks[18], (FOX_HEADS,), jnp.float32),
        'fox_k_gain': g(ks[19], (FOX_HEAD_DIM,)),
        'w_out': w(ks[20], (DEPTH, MIX_WIDTH, D_MODEL), MIX_WIDTH),
        'ffn2_norm': g(ks[21], (DEPTH, D_MODEL)),
        'ffn2_w_gate': w(ks[22], (DEPTH, D_MODEL, D_FF), D_MODEL),
        'ffn2_w_up': w(ks[23], (DEPTH, D_MODEL, D_FF), D_MODEL),
        'ffn2_w_down': w(ks[24], (DEPTH, D_FF, D_MODEL), D_FF),
    }


def reference(x, mem, ffn1_norm, ffn1_w_gate, ffn1_w_up, ffn1_w_down, mix_norm, mem_norm,
              w_mem_kv, mem_q_gain, mem_k_gain, w_in_a, hgrn_lb_logits, hgrn_o_gain,
              w_in_b, fox_q_gain, kv_norm, w_kv, fox_f_bias, fox_k_gain, w_out,
              ffn2_norm, ffn2_w_gate, ffn2_w_up, ffn2_w_down):
    lb = jnp.cumsum(jax.nn.softmax(hgrn_lb_logits.astype(jnp.float32), axis=0), axis=0)
    lb = lb - lb[0:1]
    k_sh = v_sh = cum_log_f = None
    for l in range(DEPTH):
        x = x + 0.5 * swiglu(rms_norm(x, ffn1_norm[l]), ffn1_w_gate[l], ffn1_w_up[l], ffn1_w_down[l])
        h = rms_norm(x, mix_norm[l])
        mem_n = rms_norm(mem, mem_norm[l])
        if l < N_A_LAYERS:
            proj = h @ w_in_a[l]
            q_raw = proj[..., :MAIN_WIDTH]
            f_raw = proj[..., MAIN_WIDTH:2 * MAIN_WIDTH]
            i_raw = proj[..., 2 * MAIN_WIDTH:3 * MAIN_WIDTH]
            g_raw = proj[..., 3 * MAIN_WIDTH:4 * MAIN_WIDTH]
            qm_raw = proj[..., 4 * MAIN_WIDTH:]
            f = lb[l] + (1.0 - lb[l]) * jax.nn.sigmoid(f_raw.astype(jnp.float32))
            q = jax.nn.silu(q_raw.astype(jnp.float32))
            o = hgrn2_recurrence(split_heads(q, HG_HEADS), split_heads(1.0 - f, HG_HEADS),
                                 split_heads(i_raw.astype(jnp.float32), HG_HEADS),
                                 split_heads(jnp.log(f), HG_HEADS))
            main = merge_heads(rms_norm(o, hgrn_o_gain[l])) * jax.nn.silu(g_raw.astype(jnp.float32))
        else:
            j = l - N_A_LAYERS
            proj = h @ w_in_b[j]
            q = rms_norm(split_heads(proj[..., :MAIN_WIDTH], FOX_HEADS), fox_q_gain[j])
            gate = proj[..., MAIN_WIDTH:2 * MAIN_WIDTH]
            qm_raw = proj[..., 2 * MAIN_WIDTH:]
            o = forgetting_attention(q, k_sh, v_sh, cum_log_f)
            main = merge_heads(o) * jax.nn.sigmoid(gate)
        mem_o = memory_attention(qm_raw, mem_n, w_mem_kv[l], mem_q_gain[l], mem_k_gain[l])
        mixed = jnp.concatenate([main.astype(x.dtype), mem_o.astype(x.dtype)], axis=-1)
        x = x + mixed @ w_out[l]
        x = x + 0.5 * swiglu(rms_norm(x, ffn2_norm[l]), ffn2_w_gate[l], ffn2_w_up[l], ffn2_w_down[l])
        if l == N_A_LAYERS - 1:
            kvf = rms_norm(x, kv_norm) @ w_kv
            k_sh = rms_norm(split_heads(kvf[..., :MAIN_WIDTH], FOX_HEADS), fox_k_gain)
            v_sh = split_heads(kvf[..., MAIN_WIDTH:2 * MAIN_WIDTH], FOX_HEADS)
            log_f = jax.nn.log_sigmoid(kvf[..., 2 * MAIN_WIDTH:].astype(jnp.float32) + fox_f_bias.astype(jnp.float32))
            cum_log_f = jnp.cumsum(log_f.transpose(0, 2, 1), axis=-1)
    return x
```

```python
import functools

import jax
import jax.numpy as jnp
from jax import lax
from jax.experimental import pallas as pl
from jax.experimental.pallas import tpu as pltpu

F32 = jnp.float32
BF16 = jnp.bfloat16

D_MODEL = 1024
BATCH = 8
SEQ = 2048
DEPTH = 4
N_MEM = 256
N_A_LAYERS = DEPTH // 2
MAIN_WIDTH = 768
MEM_WIDTH = 256
HG_HEAD_DIM = 128
HG_HEADS = MAIN_WIDTH // HG_HEAD_DIM
FOX_HEAD_DIM = 64
FOX_HEADS = MAIN_WIDTH // FOX_HEAD_DIM
MEM_HEADS = 4
MEM_HEAD_DIM = MEM_WIDTH // MEM_HEADS
D_FF = 2816
EPS = 1e-6
TOKENS = BATCH * SEQ

V7X_LANES = 128
V7X_VMEM_BYTES = 64 * 1024 * 1024

ROW_TILE = 512
FF_CHUNK = 256
HG_CHUNK = 64
HG_SUB = 16
FOX_TILE = 512
MEM_Q_TILE = 512
KV_TILE = 512
VMEM_LIMIT = 56 * 1024 * 1024


def _rms(x, gain):
    ms = jnp.mean(x * x, axis=-1, keepdims=True)
    return x * lax.rsqrt(ms + EPS) * gain


def _split_dot(x, rhs01, parts):
    acc = None
    rem = x
    for p in range(parts):
        piece = rem.astype(BF16)
        term = jnp.dot(piece, rhs01, preferred_element_type=F32)
        acc = term if acc is None else acc + term
        if p + 1 < parts:
            rem = rem - piece.astype(F32)
    return acc


def _group_ones(width, group):
    r = lax.broadcasted_iota(jnp.int32, (width, width), 0) // group
    c = lax.broadcasted_iota(jnp.int32, (width, width), 1) // group
    return (r == c).astype(BF16)


def _group_rms(x, gain_tiled, group):
    width = x.shape[-1]
    ss = _split_dot(x * x, _group_ones(width, group), 2)
    return x * lax.rsqrt(ss * (1.0 / group) + EPS) * gain_tiled


def _resident(shape):
    nd = len(shape)
    return pl.BlockSpec(shape, lambda *_: (0,) * nd, pipeline_mode=pl.Buffered(1))


def _params(sem):
    return pltpu.CompilerParams(dimension_semantics=sem, vmem_limit_bytes=VMEM_LIMIT)


def _swiglu_rows(x, gain_ref, wg_ref, wu_ref, wd_ref, h_ref):
    n = _rms(x, gain_ref[...]).astype(BF16)
    for c in range(D_FF // FF_CHUNK):
        sl = slice(c * FF_CHUNK, (c + 1) * FF_CHUNK)
        g = jnp.dot(n, wg_ref[:, sl], preferred_element_type=F32)
        u = jnp.dot(n, wu_ref[:, sl], preferred_element_type=F32)
        h_ref[:, sl] = (g * jax.nn.sigmoid(g) * u).astype(BF16)
    y = jnp.dot(h_ref[...], wd_ref[...], preferred_element_type=F32)
    return x + 0.5 * y


def _ffn_kernel(x_ref, gain_ref, wg_ref, wu_ref, wd_ref, o_ref, h_ref):
    o_ref[...] = _swiglu_rows(x_ref[...], gain_ref, wg_ref, wu_ref, wd_ref, h_ref)


def _mix_ffn_kernel(x_ref, main_ref, memo_ref, wo_main_ref, wo_mem_ref,
                    gain_ref, wg_ref, wu_ref, wd_ref, o_ref, h_ref):
    x = (x_ref[...]
         + jnp.dot(main_ref[...], wo_main_ref[...], preferred_element_type=F32)
         + jnp.dot(memo_ref[...], wo_mem_ref[...], preferred_element_type=F32))
    o_ref[...] = _swiglu_rows(x, gain_ref, wg_ref, wu_ref, wd_ref, h_ref)


def _row_spec(width):
    return pl.BlockSpec((ROW_TILE, width), lambda i: (i, 0))


_FFN_WEIGHT_SPECS = [
    _resident((1, D_MODEL)),
    _resident((D_MODEL, D_FF)),
    _resident((D_MODEL, D_FF)),
    _resident((D_FF, D_MODEL)),
]


def _ffn(x, gain, wg, wu, wd):
    return pl.pallas_call(
        _ffn_kernel,
        grid=(TOKENS // ROW_TILE,),
        in_specs=[_row_spec(D_MODEL)] + _FFN_WEIGHT_SPECS,
        out_specs=_row_spec(D_MODEL),
        out_shape=jax.ShapeDtypeStruct((TOKENS, D_MODEL), F32),
        scratch_shapes=[pltpu.VMEM((ROW_TILE, D_FF), BF16)],
        compiler_params=_params(("parallel",)),
        name="ffn",
    )(x, gain, wg, wu, wd)


def _mix_ffn(x, main, memo, wo_main, wo_mem, gain, wg, wu, wd):
    return pl.pallas_call(
        _mix_ffn_kernel,
        grid=(TOKENS // ROW_TILE,),
        in_specs=[_row_spec(D_MODEL), _row_spec(MAIN_WIDTH), _row_spec(MEM_WIDTH),
                  _resident((MAIN_WIDTH, D_MODEL)), _resident((MEM_WIDTH, D_MODEL))]
        + _FFN_WEIGHT_SPECS,
        out_specs=_row_spec(D_MODEL),
        out_shape=jax.ShapeDtypeStruct((TOKENS, D_MODEL), F32),
        scratch_shapes=[pltpu.VMEM((ROW_TILE, D_FF), BF16)],
        compiler_params=_params(("parallel",)),
        name="mix_ffn",
    )(x, main, memo, wo_main, wo_mem, gain, wg, wu, wd)


def _inproj_a_kernel(layer, x_ref, gain_ref, w_ref, lbl_ref,
                     q_ref, lf_ref, i_ref, g_ref, qm_ref):
    n = _rms(x_ref[...], gain_ref[...]).astype(BF16)
    rows = [lbl_ref[r:r + 1, :] for r in range(N_A_LAYERS)]
    top = functools.reduce(jnp.maximum, rows)
    exps = [jnp.exp(r - top) for r in rows]
    total = functools.reduce(jnp.add, exps)
    lb = jnp.zeros_like(top)
    for r in range(1, layer + 1):
        lb = lb + exps[r] / total

    w = MAIN_WIDTH
    q_raw = jnp.dot(n, w_ref[:, 0:w], preferred_element_type=F32)
    q_ref[...] = q_raw * jax.nn.sigmoid(q_raw)
    f_raw = jnp.dot(n, w_ref[:, w:2 * w], preferred_element_type=F32)
    lf_ref[...] = jnp.log(lb + (1.0 - lb) * jax.nn.sigmoid(f_raw))
    i_ref[...] = jnp.dot(n, w_ref[:, 2 * w:3 * w], preferred_element_type=F32).astype(BF16)
    g_raw = jnp.dot(n, w_ref[:, 3 * w:4 * w], preferred_element_type=F32)
    g_ref[...] = g_raw * jax.nn.sigmoid(g_raw)
    qm_ref[...] = jnp.dot(n, w_ref[:, 4 * w:], preferred_element_type=F32)


def _inproj_a(layer, x, gain, w_in, lb_logits):
    a_in = 4 * MAIN_WIDTH + MEM_WIDTH
    return pl.pallas_call(
        functools.partial(_inproj_a_kernel, layer),
        grid=(TOKENS // ROW_TILE,),
        in_specs=[_row_spec(D_MODEL), _resident((1, D_MODEL)), _resident((D_MODEL, a_in)),
                  _resident((N_A_LAYERS, MAIN_WIDTH))],
        out_specs=[_row_spec(MAIN_WIDTH), _row_spec(MAIN_WIDTH), _row_spec(MAIN_WIDTH),
                   _row_spec(MAIN_WIDTH), _row_spec(MEM_WIDTH)],
        out_shape=[jax.ShapeDtypeStruct((TOKENS, MAIN_WIDTH), F32),
                   jax.ShapeDtypeStruct((TOKENS, MAIN_WIDTH), F32),
                   jax.ShapeDtypeStruct((TOKENS, MAIN_WIDTH), BF16),
                   jax.ShapeDtypeStruct((TOKENS, MAIN_WIDTH), F32),
                   jax.ShapeDtypeStruct((TOKENS, MEM_WIDTH), F32)],
        compiler_params=_params(("parallel",)),
        name="inproj_a",
    )(x, gain, w_in, lb_logits)


def _inproj_b_kernel(x_ref, gain_ref, w_ref, qg_ref, q_ref, gate_ref, qm_ref):
    n = _rms(x_ref[...], gain_ref[...]).astype(BF16)
    w = MAIN_WIDTH
    scale = FOX_HEAD_DIM ** -0.5
    for c in range(w // 256):
        sl = slice(c * 256, (c + 1) * 256)
        q_raw = jnp.dot(n, w_ref[:, sl], preferred_element_type=F32)
        q_ref[:, sl] = (_group_rms(q_raw, qg_ref[:, sl], FOX_HEAD_DIM) * scale).astype(BF16)
    gate = jnp.dot(n, w_ref[:, w:2 * w], preferred_element_type=F32)
    gate_ref[...] = jax.nn.sigmoid(gate)
    qm_ref[...] = jnp.dot(n, w_ref[:, 2 * w:], preferred_element_type=F32)


def _inproj_b(x, gain, w_in, q_gain_tiled):
    b_in = 2 * MAIN_WIDTH + MEM_WIDTH
    return pl.pallas_call(
        _inproj_b_kernel,
        grid=(TOKENS // ROW_TILE,),
        in_specs=[_row_spec(D_MODEL), _resident((1, D_MODEL)), _resident((D_MODEL, b_in)),
                  _resident((1, MAIN_WIDTH))],
        out_specs=[_row_spec(MAIN_WIDTH), _row_spec(MAIN_WIDTH), _row_spec(MEM_WIDTH)],
        out_shape=[jax.ShapeDtypeStruct((TOKENS, MAIN_WIDTH), BF16),
                   jax.ShapeDtypeStruct((TOKENS, MAIN_WIDTH), F32),
                   jax.ShapeDtypeStruct((TOKENS, MEM_WIDTH), F32)],
        compiler_params=_params(("parallel",)),
        name="inproj_b",
    )(x, gain, w_in, q_gain_tiled)


def _hgrn_kernel(q_ref, lf_ref, v_ref, g_ref, gain_ref, o_ref,
                 st_ref, cpad_ref, kpad_ref, vpad_ref):
    C, SB, HD = HG_CHUNK, HG_SUB, HG_HEAD_DIM
    st_ref[...] = jnp.zeros_like(st_ref)
    zpad = jnp.zeros((SB, HD), F32)
    cpad_ref[0:SB, :] = zpad
    kpad_ref[0:SB, :] = zpad
    vpad_ref[0:SB, :] = zpad

    r_i = lax.broadcasted_iota(jnp.int32, (C, C), 0)
    c_i = lax.broadcasted_iota(jnp.int32, (C, C), 1)
    tril = (r_i >= c_i).astype(BF16)
    ones = jnp.ones((HD, HD), BF16)
    row_in_block = lax.broadcasted_iota(jnp.int32, (C, HD), 0) % SB
    gain = gain_ref[...]

    def body(n, carry):
        r0 = pl.multiple_of(n * C, C)
        lf = lf_ref[0, pl.ds(r0, C), :]
        q = q_ref[0, pl.ds(r0, C), :]
        v16 = v_ref[0, pl.ds(r0, C), :]
        v = v16.astype(F32)
        k = 1.0 - jnp.exp(lf)
        c = None
        rem = lf
        for p in range(3):
            piece = rem.astype(BF16)
            term = jnp.dot(tril, piece, preferred_element_type=F32)
            c = term if c is None else c + term
            rem = rem - piece.astype(F32)
        c_end = c[C - 1:C, :]

        st = st_ref[...]
        qe = (q * jnp.exp(c)).astype(BF16)
        o = lax.dot_general(qe, st.astype(BF16), (((1,), (1,)), ((), ())),
                            preferred_element_type=F32)
        kd = (k * jnp.exp(c_end - c)).astype(BF16)
        upd = lax.dot_general(v16, kd, (((0,), (0,)), ((), ())),
                              preferred_element_type=F32)
        st_ref[...] = st * jnp.exp(c_end) + upd

        off = [jnp.zeros((SB, HD), F32)]
        for b in range(1, C // SB):
            lo = b * SB
            ref = c[lo - 1:lo, :]
            qb = (q[lo:lo + SB, :] * jnp.exp(c[lo:lo + SB, :] - ref)).astype(BF16)
            kb = (k[0:lo, :] * jnp.exp(ref - c[0:lo, :])).astype(BF16)
            p_b = lax.dot_general(qb, kb, (((1,), (1,)), ((), ())),
                                  preferred_element_type=F32)
            off.append(jnp.dot(p_b.astype(BF16), v16[0:lo, :], preferred_element_type=F32))
        o = o + jnp.concatenate(off, axis=0)

        cpad_ref[SB:SB + C, :] = c
        kpad_ref[SB:SB + C, :] = k
        vpad_ref[SB:SB + C, :] = v
        for d in range(SB):
            cs = cpad_ref[SB - d:SB - d + C, :]
            ks = kpad_ref[SB - d:SB - d + C, :]
            vs = vpad_ref[SB - d:SB - d + C, :]
            wgt = jnp.where(row_in_block >= d, q * ks * jnp.exp(c - cs), 0.0)
            s = jnp.dot(wgt.astype(BF16), ones, preferred_element_type=F32)
            o = o + s * vs

        o_n = _rms(o, gain)
        o_ref[0, pl.ds(r0, C), :] = (o_n * g_ref[0, pl.ds(r0, C), :]).astype(BF16)
        return carry

    lax.fori_loop(0, SEQ // C, body, 0)


def _hgrn(q, lf, v, g, o_gain):
    spec = pl.BlockSpec((1, SEQ, HG_HEAD_DIM), lambda b, h: (b, 0, h))
    return pl.pallas_call(
        _hgrn_kernel,
        grid=(BATCH, HG_HEADS),
        in_specs=[spec, spec, spec, spec, _resident((1, HG_HEAD_DIM))],
        out_specs=spec,
        out_shape=jax.ShapeDtypeStruct((BATCH, SEQ, MAIN_WIDTH), BF16),
        scratch_shapes=[pltpu.VMEM((HG_HEAD_DIM, HG_HEAD_DIM), F32),
                        pltpu.VMEM((HG_SUB + HG_CHUNK, HG_HEAD_DIM), F32),
                        pltpu.VMEM((HG_SUB + HG_CHUNK, HG_HEAD_DIM), F32),
                        pltpu.VMEM((HG_SUB + HG_CHUNK, HG_HEAD_DIM), F32)],
        compiler_params=_params(("parallel", "parallel")),
        name="hgrn",
    )(q, lf, v, g, o_gain)


def _kv_kernel(x_ref, gain_ref, wk_ref, wv_ref, wf_ref, fb_ref, kg_ref,
               k_ref, v_ref, d_ref, carry_ref):
    @pl.when(pl.program_id(1) == 0)
    def _():
        carry_ref[...] = jnp.zeros_like(carry_ref)

    n = _rms(x_ref[0], gain_ref[...]).astype(BF16)
    for c in range(MAIN_WIDTH // 256):
        sl = slice(c * 256, (c + 1) * 256)
        k_raw = jnp.dot(n, wk_ref[:, sl], preferred_element_type=F32)
        k_ref[0, :, sl] = _group_rms(k_raw, kg_ref[:, sl], FOX_HEAD_DIM).astype(BF16)
    v_ref[0] = jnp.dot(n, wv_ref[...], preferred_element_type=F32).astype(BF16)
    z = jnp.dot(n, wf_ref[...], preferred_element_type=F32) + fb_ref[...]
    log_f = jnp.minimum(z, 0.0) - jnp.log1p(jnp.exp(-jnp.abs(z)))
    r_i = lax.broadcasted_iota(jnp.int32, (KV_TILE, KV_TILE), 0)
    c_i = lax.broadcasted_iota(jnp.int32, (KV_TILE, KV_TILE), 1)
    tril = (r_i >= c_i).astype(BF16)
    cum = None
    rem = log_f
    for p in range(3):
        piece = rem.astype(BF16)
        term = jnp.dot(tril, piece, preferred_element_type=F32)
        cum = term if cum is None else cum + term
        rem = rem - piece.astype(F32)
    cum = cum + carry_ref[...]
    d_ref[0] = cum
    carry_ref[...] = cum[KV_TILE - 1:KV_TILE, :]


def _kv(x, gain, wk, wv, wf, f_bias, k_gain_tiled):
    row = lambda w: pl.BlockSpec((1, KV_TILE, w), lambda b, i: (b, i, 0))
    return pl.pallas_call(
        _kv_kernel,
        grid=(BATCH, SEQ // KV_TILE),
        in_specs=[row(D_MODEL), _resident((1, D_MODEL)), _resident((D_MODEL, MAIN_WIDTH)),
                  _resident((D_MODEL, MAIN_WIDTH)), _resident((D_MODEL, V7X_LANES)),
                  _resident((1, V7X_LANES)), _resident((1, MAIN_WIDTH))],
        out_specs=[row(MAIN_WIDTH), row(MAIN_WIDTH), row(V7X_LANES)],
        out_shape=[jax.ShapeDtypeStruct((BATCH, SEQ, MAIN_WIDTH), BF16),
                   jax.ShapeDtypeStruct((BATCH, SEQ, MAIN_WIDTH), BF16),
                   jax.ShapeDtypeStruct((BATCH, SEQ, V7X_LANES), F32)],
        scratch_shapes=[pltpu.VMEM((1, V7X_LANES), F32)],
        compiler_params=_params(("parallel", "arbitrary")),
        name="kv",
    )(x, gain, wk, wv, wf, f_bias, k_gain_tiled)


def _fox_kernel(q_ref, k_ref, v_ref, dcol_ref, drow_ref, gate_ref, o_ref,
                qa_ref, m_ref, l_ref, acc_ref):
    T, HD = FOX_TILE, FOX_HEAD_DIM
    pair = pl.program_id(1)
    qi = pl.program_id(2)
    lane = lax.broadcasted_iota(jnp.int32, (T, 2 * HD), 1)
    head_of_lane = lane // HD
    q = q_ref[0]
    dcol = dcol_ref[0]
    lane_d = lax.broadcasted_iota(jnp.int32, dcol.shape, 1)
    d_t = []
    for hh in range(2):
        qa_ref[hh] = jnp.where(head_of_lane == hh, q, jnp.zeros_like(q))
        d_t.append(jnp.sum(jnp.where(lane_d == 2 * pair + hh, dcol, 0.0),
                           axis=-1, keepdims=True))
        m_ref[hh] = jnp.full((T, 1), -jnp.inf, F32)
        l_ref[hh] = jnp.zeros((T, 1), F32)
    acc_ref[...] = jnp.zeros_like(acc_ref)
    causal = (lax.broadcasted_iota(jnp.int32, (T, T), 1)
              <= lax.broadcasted_iota(jnp.int32, (T, T), 0))

    def block(j, masked):
        c0 = pl.multiple_of(j * T, T)
        kb = k_ref[0, pl.ds(c0, T), :]
        vb = v_ref[0, pl.ds(c0, T), :]
        alphas, pvs = [], []
        for hh in range(2):
            s = lax.dot_general(qa_ref[hh], kb, (((1,), (1,)), ((), ())),
                                preferred_element_type=F32)
            s = s + (d_t[hh] - drow_ref[0, hh, :, pl.ds(c0, T)])
            if masked:
                s = jnp.where(causal, s, -jnp.inf)
            m_old = m_ref[hh]
            m_new = jnp.maximum(m_old, jnp.max(s, axis=-1, keepdims=True))
            p = jnp.exp(s - m_new)
            alpha = jnp.exp(m_old - m_new)
            l_ref[hh] = alpha * l_ref[hh] + jnp.sum(p, axis=-1, keepdims=True)
            m_ref[hh] = m_new
            alphas.append(alpha)
            pvs.append(jnp.dot(p.astype(BF16), vb, preferred_element_type=F32))
        first = head_of_lane == 0
        acc_ref[...] = (jnp.where(first, alphas[0], alphas[1]) * acc_ref[...]
                        + jnp.where(first, pvs[0], pvs[1]))

    def loop_body(j, carry):
        block(j, False)
        return carry

    lax.fori_loop(0, qi, loop_body, 0)
    block(qi, True)
    inv_l = jnp.where(head_of_lane == 0, 1.0 / l_ref[0], 1.0 / l_ref[1])
    o_ref[0] = (acc_ref[...] * inv_l * gate_ref[0]).astype(BF16)


def _fox(q, k, v, dcol, drow, gate):
    T = FOX_TILE
    qspec = pl.BlockSpec((1, T, 2 * FOX_HEAD_DIM), lambda b, p, i: (b, i, p))
    kvspec = pl.BlockSpec((1, SEQ, 2 * FOX_HEAD_DIM), lambda b, p, i: (b, 0, p))
    return pl.pallas_call(
        _fox_kernel,
        grid=(BATCH, FOX_HEADS // 2, SEQ // T),
        in_specs=[qspec, kvspec, kvspec,
                  pl.BlockSpec((1, T, V7X_LANES), lambda b, p, i: (b, i, 0)),
                  pl.BlockSpec((1, 2, 1, SEQ), lambda b, p, i: (b, p, 0, 0)),
                  qspec],
        out_specs=qspec,
        out_shape=jax.ShapeDtypeStruct((BATCH, SEQ, MAIN_WIDTH), BF16),
        scratch_shapes=[pltpu.VMEM((2, T, 2 * FOX_HEAD_DIM), BF16),
                        pltpu.VMEM((2, T, 1), F32),
                        pltpu.VMEM((2, T, 1), F32),
                        pltpu.VMEM((T, 2 * FOX_HEAD_DIM), F32)],
        compiler_params=_params(("parallel", "parallel", "parallel")),
        name="fox",
    )(q, k, v, dcol, drow, gate)


def _memattn_kernel(qm_ref, mem_ref, mgain_ref, wkv_ref, qg_ref, kg_ref, o_ref,
                    km_ref, vm_ref):
    W, HD = MEM_WIDTH, MEM_HEAD_DIM

    @pl.when(pl.program_id(1) == 0)
    def _():
        mem_n = _rms(mem_ref[0], mgain_ref[...]).astype(BF16)
        kv = jnp.dot(mem_n, wkv_ref[...], preferred_element_type=F32)
        km_ref[...] = _group_rms(kv[:, :W], kg_ref[...], HD).astype(BF16)
        vm_ref[...] = kv[:, W:].astype(BF16)

    scale = HD ** -0.5
    qn = (_group_rms(qm_ref[0], qg_ref[...], HD) * scale).astype(BF16)
    head_of_lane = lax.broadcasted_iota(jnp.int32, qn.shape, 1) // HD
    km = km_ref[...]
    vm = vm_ref[...]
    out = jnp.zeros(qn.shape, F32)
    for h in range(MEM_HEADS):
        mine = head_of_lane == h
        s = lax.dot_general(jnp.where(mine, qn, jnp.zeros_like(qn)), km,
                            (((1,), (1,)), ((), ())), preferred_element_type=F32)
        p = jnp.exp(s - jnp.max(s, axis=-1, keepdims=True))
        inv = 1.0 / jnp.sum(p, axis=-1, keepdims=True)
        pv = jnp.dot(p.astype(BF16), vm, preferred_element_type=F32)
        out = out + jnp.where(mine, pv * inv, 0.0)
    o_ref[0] = out.astype(BF16)


def _memattn(qm, mem, mem_gain, w_mem_kv, q_gain_tiled, k_gain_tiled):
    T = MEM_Q_TILE
    qspec = pl.BlockSpec((1, T, MEM_WIDTH), lambda b, i: (b, i, 0))
    return pl.pallas_call(
        _memattn_kernel,
        grid=(BATCH, SEQ // T),
        in_specs=[qspec,
                  pl.BlockSpec((1, N_MEM, D_MODEL), lambda b, i: (b, 0, 0)),
                  _resident((1, D_MODEL)), _resident((D_MODEL, 2 * MEM_WIDTH)),
                  _resident((1, MEM_WIDTH)), _resident((1, MEM_WIDTH))],
        out_specs=qspec,
        out_shape=jax.ShapeDtypeStruct((BATCH, SEQ, MEM_WIDTH), BF16),
        scratch_shapes=[pltpu.VMEM((N_MEM, MEM_WIDTH), BF16),
                        pltpu.VMEM((N_MEM, MEM_WIDTH), BF16)],
        compiler_params=_params(("parallel", "arbitrary")),
        name="memattn",
    )(qm, mem, mem_gain, w_mem_kv, q_gain_tiled, k_gain_tiled)


def kernel(x, mem, ffn1_norm, ffn1_w_gate, ffn1_w_up, ffn1_w_down, mix_norm, mem_norm,
           w_mem_kv, mem_q_gain, mem_k_gain, w_in_a, hgrn_lb_logits, hgrn_o_gain,
           w_in_b, fox_q_gain, kv_norm, w_kv, fox_f_bias, fox_k_gain, w_out,
           ffn2_norm, ffn2_w_gate, ffn2_w_up, ffn2_w_down):
    bf = lambda t: t.astype(BF16)
    row = lambda t: t.reshape(1, -1).astype(F32)
    x = x.reshape(TOKENS, D_MODEL).astype(F32)
    mem = mem.astype(F32)
    lb_logits = hgrn_lb_logits.astype(F32)

    for l in range(DEPTH):
        x = _ffn(x, row(ffn1_norm[l]), bf(ffn1_w_gate[l]), bf(ffn1_w_up[l]), bf(ffn1_w_down[l]))
        if l < N_A_LAYERS:
            qs, lf, iv, sg, qm = _inproj_a(l, x, row(mix_norm[l]), bf(w_in_a[l]), lb_logits)
            to3 = lambda t: t.reshape(BATCH, SEQ, MAIN_WIDTH)
            main = _hgrn(to3(qs), to3(lf), to3(iv), to3(sg), row(hgrn_o_gain[l]))
        else:
            j = l - N_A_LAYERS
            qn, gate, qm = _inproj_b(x, row(mix_norm[l]), bf(w_in_b[j]),
                                     row(jnp.tile(fox_q_gain[j], FOX_HEADS)))
            to3 = lambda t: t.reshape(BATCH, SEQ, MAIN_WIDTH)
            main = _fox(to3(qn), k_sh, v_sh, d_col, d_row, to3(gate))
        memo = _memattn(qm.reshape(BATCH, SEQ, MEM_WIDTH), mem, row(mem_norm[l]),
                        bf(w_mem_kv[l]), row(jnp.tile(mem_q_gain[l], MEM_HEADS)),
                        row(jnp.tile(mem_k_gain[l], MEM_HEADS)))
        x = _mix_ffn(x, main.reshape(TOKENS, MAIN_WIDTH), memo.reshape(TOKENS, MEM_WIDTH),
                     bf(w_out[l, :MAIN_WIDTH]), bf(w_out[l, MAIN_WIDTH:]),
                     row(ffn2_norm[l]), bf(ffn2_w_gate[l]), bf(ffn2_w_up[l]), bf(ffn2_w_down[l]))
        if l == N_A_LAYERS - 1:
            pad = V7X_LANES - FOX_HEADS
            wf = jnp.pad(w_kv[:, 2 * MAIN_WIDTH:], ((0, 0), (0, pad)))
            fb = jnp.pad(fox_f_bias.astype(F32), (0, pad)).reshape(1, -1)
            k_sh, v_sh, d_col = _kv(x.reshape(BATCH, SEQ, D_MODEL), row(kv_norm),
                                    bf(w_kv[:, :MAIN_WIDTH]),
                                    bf(w_kv[:, MAIN_WIDTH:2 * MAIN_WIDTH]), bf(wf), fb,
                                    row(jnp.tile(fox_k_gain, FOX_HEADS)))
            d_row = jnp.transpose(d_col[:, :, :FOX_HEADS], (0, 2, 1)).reshape(
                BATCH, FOX_HEADS, 1, SEQ)
    return x.reshape(BATCH, SEQ, D_MODEL)
```

```python
import functools

import jax
import jax.numpy as jnp
from jax import lax
from jax.experimental import pallas as pl
from jax.experimental.pallas import tpu as pltpu

F32 = jnp.float32
BF16 = jnp.bfloat16

D_MODEL = 1024
BATCH = 8
SEQ = 2048
DEPTH = 4
N_MEM = 256
N_A_LAYERS = DEPTH // 2
MAIN_WIDTH = 768
MEM_WIDTH = 256
HG_HEAD_DIM = 128
HG_HEADS = MAIN_WIDTH // HG_HEAD_DIM
FOX_HEAD_DIM = 64
FOX_HEADS = MAIN_WIDTH // FOX_HEAD_DIM
MEM_HEADS = 4
MEM_HEAD_DIM = MEM_WIDTH // MEM_HEADS
D_FF = 2816
EPS = 1e-6
TOKENS = BATCH * SEQ

V7X_LANES = 128
V7X_VMEM_BYTES = 64 * 1024 * 1024

ROW_TILE = 512
FF_CHUNK = 256
HG_CHUNK = 64
HG_SUB = 16
HG_SAFE_DECAY = 60.0
HG_GROUP = 4
FOX_TILE = 512
FOX_ONES_ROWS = 16
MEM_Q_TILE = 512
KV_TILE = 512
VMEM_LIMIT = 56 * 1024 * 1024


def _rms(x, gain):
    ms = jnp.mean(x * x, axis=-1, keepdims=True)
    return x * lax.rsqrt(ms + EPS) * gain


def _split_dot(x, rhs01, parts):
    acc = None
    rem = x
    for p in range(parts):
        piece = rem.astype(BF16)
        term = jnp.dot(piece, rhs01, preferred_element_type=F32)
        acc = term if acc is None else acc + term
        if p + 1 < parts:
            rem = rem - piece.astype(F32)
    return acc


def _group_ones(width, group):
    r = lax.broadcasted_iota(jnp.int32, (width, width), 0) // group
    c = lax.broadcasted_iota(jnp.int32, (width, width), 1) // group
    return (r == c).astype(BF16)


def _group_rms(x, gain_tiled, group):
    width = x.shape[-1]
    ss = _split_dot(x * x, _group_ones(width, group), 2)
    return x * lax.rsqrt(ss * (1.0 / group) + EPS) * gain_tiled


def _resident(shape):
    nd = len(shape)
    return pl.BlockSpec(shape, lambda *_: (0,) * nd, pipeline_mode=pl.Buffered(1))


def _params(sem):
    return pltpu.CompilerParams(dimension_semantics=sem, vmem_limit_bytes=VMEM_LIMIT)


def _swiglu_rows(x, gain_ref, wg_ref, wu_ref, wd_ref, h_ref):
    n = _rms(x, gain_ref[...]).astype(BF16)
    for c in range(D_FF // FF_CHUNK):
        sl = slice(c * FF_CHUNK, (c + 1) * FF_CHUNK)
        g = jnp.dot(n, wg_ref[:, sl], preferred_element_type=F32)
        u = jnp.dot(n, wu_ref[:, sl], preferred_element_type=F32)
        h_ref[:, sl] = (g * jax.nn.sigmoid(g) * u).astype(BF16)
    y = jnp.dot(h_ref[...], wd_ref[...], preferred_element_type=F32)
    return x + 0.5 * y


def _ffn_kernel(x_ref, gain_ref, wg_ref, wu_ref, wd_ref, o_ref, h_ref):
    o_ref[...] = _swiglu_rows(x_ref[...], gain_ref, wg_ref, wu_ref, wd_ref, h_ref)


def _mix_ffn_kernel(x_ref, main_ref, memo_ref, wo_main_ref, wo_mem_ref,
                    gain_ref, wg_ref, wu_ref, wd_ref, o_ref, h_ref):
    x = (x_ref[...]
         + jnp.dot(main_ref[...], wo_main_ref[...], preferred_element_type=F32)
         + jnp.dot(memo_ref[...], wo_mem_ref[...], preferred_element_type=F32))
    o_ref[...] = _swiglu_rows(x, gain_ref, wg_ref, wu_ref, wd_ref, h_ref)


def _row_spec(width):
    return pl.BlockSpec((ROW_TILE, width), lambda i: (i, 0))


_FFN_WEIGHT_SPECS = [
    _resident((1, D_MODEL)),
    _resident((D_MODEL, D_FF)),
    _resident((D_MODEL, D_FF)),
    _resident((D_FF, D_MODEL)),
]


def _ffn(x, gain, wg, wu, wd):
    return pl.pallas_call(
        _ffn_kernel,
        grid=(TOKENS // ROW_TILE,),
        in_specs=[_row_spec(D_MODEL)] + _FFN_WEIGHT_SPECS,
        out_specs=_row_spec(D_MODEL),
        out_shape=jax.ShapeDtypeStruct((TOKENS, D_MODEL), F32),
        scratch_shapes=[pltpu.VMEM((ROW_TILE, D_FF), BF16)],
        compiler_params=_params(("parallel",)),
        name="ffn",
    )(x, gain, wg, wu, wd)


def _mix_ffn(x, main, memo, wo_main, wo_mem, gain, wg, wu, wd):
    return pl.pallas_call(
        _mix_ffn_kernel,
        grid=(TOKENS // ROW_TILE,),
        in_specs=[_row_spec(D_MODEL), _row_spec(MAIN_WIDTH), _row_spec(MEM_WIDTH),
                  _resident((MAIN_WIDTH, D_MODEL)), _resident((MEM_WIDTH, D_MODEL))]
        + _FFN_WEIGHT_SPECS,
        out_specs=_row_spec(D_MODEL),
        out_shape=jax.ShapeDtypeStruct((TOKENS, D_MODEL), F32),
        scratch_shapes=[pltpu.VMEM((ROW_TILE, D_FF), BF16)],
        compiler_params=_params(("parallel",)),
        name="mix_ffn",
    )(x, main, memo, wo_main, wo_mem, gain, wg, wu, wd)


def _inproj_a_kernel(layer, x_ref, gain_ref, w_ref, lbl_ref,
                     q_ref, lf_ref, i_ref, g_ref, qm_ref):
    n = _rms(x_ref[...], gain_ref[...]).astype(BF16)
    rows = [lbl_ref[r:r + 1, :] for r in range(N_A_LAYERS)]
    top = functools.reduce(jnp.maximum, rows)
    exps = [jnp.exp(r - top) for r in rows]
    total = functools.reduce(jnp.add, exps)
    lb = jnp.zeros_like(top)
    for r in range(1, layer + 1):
        lb = lb + exps[r] / total

    w = MAIN_WIDTH
    q_raw = jnp.dot(n, w_ref[:, 0:w], preferred_element_type=F32)
    q_ref[...] = q_raw * jax.nn.sigmoid(q_raw)
    f_raw = jnp.dot(n, w_ref[:, w:2 * w], preferred_element_type=F32)
    lf_ref[...] = jnp.log(lb + (1.0 - lb) * jax.nn.sigmoid(f_raw))
    i_ref[...] = jnp.dot(n, w_ref[:, 2 * w:3 * w], preferred_element_type=F32).astype(BF16)
    g_raw = jnp.dot(n, w_ref[:, 3 * w:4 * w], preferred_element_type=F32)
    g_ref[...] = g_raw * jax.nn.sigmoid(g_raw)
    qm_ref[...] = jnp.dot(n, w_ref[:, 4 * w:], preferred_element_type=F32)


def _inproj_a(layer, x, gain, w_in, lb_logits):
    a_in = 4 * MAIN_WIDTH + MEM_WIDTH
    return pl.pallas_call(
        functools.partial(_inproj_a_kernel, layer),
        grid=(TOKENS // ROW_TILE,),
        in_specs=[_row_spec(D_MODEL), _resident((1, D_MODEL)), _resident((D_MODEL, a_in)),
                  _resident((N_A_LAYERS, MAIN_WIDTH))],
        out_specs=[_row_spec(MAIN_WIDTH), _row_spec(MAIN_WIDTH), _row_spec(MAIN_WIDTH),
                   _row_spec(MAIN_WIDTH), _row_spec(MEM_WIDTH)],
        out_shape=[jax.ShapeDtypeStruct((TOKENS, MAIN_WIDTH), F32),
                   jax.ShapeDtypeStruct((TOKENS, MAIN_WIDTH), F32),
                   jax.ShapeDtypeStruct((TOKENS, MAIN_WIDTH), BF16),
                   jax.ShapeDtypeStruct((TOKENS, MAIN_WIDTH), F32),
                   jax.ShapeDtypeStruct((TOKENS, MEM_WIDTH), F32)],
        compiler_params=_params(("parallel",)),
        name="inproj_a",
    )(x, gain, w_in, lb_logits)


def _inproj_b_kernel(x_ref, gain_ref, w_ref, qg_ref, qbias_ref, qa_ref, gate_ref, qm_ref):
    n = _rms(x_ref[...], gain_ref[...]).astype(BF16)
    w = MAIN_WIDTH
    scale = FOX_HEAD_DIM ** -0.5 * LOG2E
    pair_w = 2 * FOX_HEAD_DIM
    for c in range(w // 256):
        sl = slice(c * 256, (c + 1) * 256)
        q_raw = jnp.dot(n, w_ref[:, sl], preferred_element_type=F32)
        q_n = (_group_rms(q_raw, qg_ref[:, sl], FOX_HEAD_DIM) * scale).astype(BF16)
        for pp in range(256 // pair_w):
            lo = c * 256 + pp * pair_w
            for hh in range(2):
                dst = 2 * lo + hh * pair_w
                qa_ref[:, dst:dst + pair_w] = _with_bias_lanes(
                    q_n[:, pp * pair_w:(pp + 1) * pair_w], qbias_ref[:, lo:lo + pair_w], hh)
    gate = jnp.dot(n, w_ref[:, w:2 * w], preferred_element_type=F32)
    gate_ref[...] = jax.nn.sigmoid(gate)
    qm_ref[...] = jnp.dot(n, w_ref[:, 2 * w:], preferred_element_type=F32)


def _inproj_b(x, gain, w_in, q_gain_tiled, q_bias):
    b_in = 2 * MAIN_WIDTH + MEM_WIDTH
    return pl.pallas_call(
        _inproj_b_kernel,
        grid=(TOKENS // ROW_TILE,),
        in_specs=[_row_spec(D_MODEL), _resident((1, D_MODEL)), _resident((D_MODEL, b_in)),
                  _resident((1, MAIN_WIDTH)), _row_spec(MAIN_WIDTH)],
        out_specs=[_row_spec(2 * MAIN_WIDTH), _row_spec(MAIN_WIDTH), _row_spec(MEM_WIDTH)],
        out_shape=[jax.ShapeDtypeStruct((TOKENS, 2 * MAIN_WIDTH), BF16),
                   jax.ShapeDtypeStruct((TOKENS, MAIN_WIDTH), F32),
                   jax.ShapeDtypeStruct((TOKENS, MEM_WIDTH), F32)],
        compiler_params=_params(("parallel",)),
        name="inproj_b",
    )(x, gain, w_in, q_gain_tiled, q_bias)


def _hgrn_kernel(q_ref, lf_ref, v_ref, g_ref, gain_ref, o_ref,
                 cpad_ref, kpad_ref, vpad_ref):
    C, SB, HD = HG_CHUNK, HG_SUB, HG_HEAD_DIM
    n_sub = C // SB
    r_i = lax.broadcasted_iota(jnp.int32, (C, C), 0)
    c_i = lax.broadcasted_iota(jnp.int32, (C, C), 1)
    tril = (r_i >= c_i).astype(BF16)
    gain = gain_ref[...]

    def intra_exact(q, k, v16, c):
        outs = [jnp.zeros((SB, HD), F32)]
        for b in range(1, n_sub):
            lo = b * SB
            ref = c[lo - 1:lo, :]
            qb = (q[lo:lo + SB, :] * jnp.exp(c[lo:lo + SB, :] - ref)).astype(BF16)
            kb = (k[0:lo, :] * jnp.exp(ref - c[0:lo, :])).astype(BF16)
            p_b = lax.dot_general(qb, kb, (((1,), (1,)), ((), ())),
                                  preferred_element_type=F32)
            outs.append(jnp.dot(p_b.astype(BF16), v16[0:lo, :], preferred_element_type=F32))
        o = jnp.concatenate(outs, axis=0)
        ones = jnp.ones((HD, HD), BF16)
        row_in_block = lax.broadcasted_iota(jnp.int32, (C, HD), 0) % SB
        cpad_ref[SB:SB + C, :] = c
        kpad_ref[SB:SB + C, :] = k
        vpad_ref[SB:SB + C, :] = v16.astype(F32)
        for d in range(SB):
            cs = cpad_ref[SB - d:SB - d + C, :]
            ks = kpad_ref[SB - d:SB - d + C, :]
            vs = vpad_ref[SB - d:SB - d + C, :]
            wgt = jnp.where(row_in_block >= d, q * ks * jnp.exp(c - cs), 0.0)
            s = jnp.dot(wgt.astype(BF16), ones, preferred_element_type=F32)
            o = o + s * vs
        return o

    def chunk(n, st, intra):
        r0 = pl.multiple_of(n * C, C)
        lf = lf_ref[0, pl.ds(r0, C), :]
        q = q_ref[0, pl.ds(r0, C), :]
        v16 = v_ref[0, pl.ds(r0, C), :]
        k = 1.0 - jnp.exp(lf)
        c = None
        rem = lf
        for p in range(3):
            piece = rem.astype(BF16)
            term = jnp.dot(tril, piece, preferred_element_type=F32)
            c = term if c is None else c + term
            rem = rem - piece.astype(F32)
        c_end = c[C - 1:C, :]

        qe = (q * jnp.exp(c)).astype(BF16)
        o = lax.dot_general(qe, st.astype(BF16), (((1,), (1,)), ((), ())),
                            preferred_element_type=F32)
        kd = (k * jnp.exp(c_end - c)).astype(BF16)
        upd = lax.dot_general(v16, kd, (((0,), (0,)), ((), ())),
                              preferred_element_type=F32)
        o = o + intra(q, k, v16, c)
        o_ref[0, pl.ds(r0, C), :] = (_rms(o, gain) * g_ref[0, pl.ds(r0, C), :]).astype(BF16)
        return st * jnp.exp(c_end) + upd

    G = HG_GROUP
    R = G * C
    gr = lax.broadcasted_iota(jnp.int32, (R, R), 0)
    gc = lax.broadcasted_iota(jnp.int32, (R, R), 1)
    tril_group = ((gr // C == gc // C) & (gr >= gc)).astype(BF16)
    sub_of_row = lax.broadcasted_iota(jnp.int32, (C, HD), 0) // SB
    causal = r_i >= c_i

    def group(i, st):
        r0 = pl.multiple_of(i * R, R)
        lf = lf_ref[0, pl.ds(r0, R), :]
        q = q_ref[0, pl.ds(r0, R), :]
        v16 = v_ref[0, pl.ds(r0, R), :]
        k = 1.0 - jnp.exp(lf)
        c = None
        rem = lf
        for p in range(3):
            piece = rem.astype(BF16)
            term = jnp.dot(tril_group, piece, preferred_element_type=F32)
            c = term if c is None else c + term
            rem = rem - piece.astype(F32)

        lhs, rhs, c_ends = [], [], []
        for g in range(G):
            cg = c[g * C:(g + 1) * C, :]
            qg = q[g * C:(g + 1) * C, :]
            kg = k[g * C:(g + 1) * C, :]
            refs = [jnp.zeros((1, HD), F32)] + [cg[b * SB - 1:b * SB, :] for b in range(1, n_sub)]
            own_ref = jnp.concatenate([jnp.broadcast_to(r, (SB, HD)) for r in refs], axis=0)
            k_own = kg * jnp.exp(own_ref - cg)
            lhs.append(jnp.concatenate(
                [(qg * jnp.exp(jnp.minimum(cg - r, 0.0))).astype(BF16) for r in refs], axis=1))
            rhs.append(jnp.concatenate(
                [jnp.where(sub_of_row == b, k_own, 0.0).astype(BF16) for b in range(n_sub)],
                axis=1))
            c_ends.append(cg[C - 1:C, :])
        c_end_rows = jnp.concatenate([jnp.broadcast_to(e, (C, HD)) for e in c_ends], axis=0)
        qe = (q * jnp.exp(c)).astype(BF16)
        kd = (k * jnp.exp(c_end_rows - c)).astype(BF16)

        scores = [lax.dot_general(lhs[g], rhs[g], (((1,), (1,)), ((), ())),
                                  preferred_element_type=F32) for g in range(G)]
        upds = [lax.dot_general(v16[g * C:(g + 1) * C, :], kd[g * C:(g + 1) * C, :],
                                (((0,), (0,)), ((), ())), preferred_element_type=F32)
                for g in range(G)]
        intra = [jnp.dot(jnp.where(causal, scores[g], 0.0).astype(BF16),
                         v16[g * C:(g + 1) * C, :], preferred_element_type=F32)
                 for g in range(G)]
        outs = []
        for g in range(G):
            inter = lax.dot_general(qe[g * C:(g + 1) * C, :], st.astype(BF16),
                                    (((1,), (1,)), ((), ())), preferred_element_type=F32)
            outs.append(inter + intra[g])
            st = st * jnp.exp(c_ends[g]) + upds[g]
        o = jnp.concatenate(outs, axis=0)
        o_ref[0, pl.ds(r0, R), :] = (_rms(o, gain) * g_ref[0, pl.ds(r0, R), :]).astype(BF16)
        return st

    def run_exact():
        def body(n, st):
            return chunk(n, st, intra_exact)
        lax.fori_loop(0, SEQ // C, body, jnp.zeros((HD, HD), F32))

    blk = lax.broadcasted_iota(jnp.int32, (SEQ // SB, SEQ), 0)
    tok = lax.broadcasted_iota(jnp.int32, (SEQ // SB, SEQ), 1) // SB
    block_decay = jnp.dot((blk == tok).astype(BF16), lf_ref[0].astype(BF16),
                          preferred_element_type=F32)
    mild = jnp.min(block_decay) >= -HG_SAFE_DECAY

    @pl.when(mild)
    def _():
        lax.fori_loop(0, SEQ // R, group, jnp.zeros((HD, HD), F32))

    @pl.when(jnp.logical_not(mild))
    def _():
        zpad = jnp.zeros((SB, HD), F32)
        cpad_ref[0:SB, :] = zpad
        kpad_ref[0:SB, :] = zpad
        vpad_ref[0:SB, :] = zpad
        run_exact()


def _hgrn(q, lf, v, g, o_gain):
    spec = pl.BlockSpec((1, SEQ, HG_HEAD_DIM), lambda b, h: (b, 0, h))
    return pl.pallas_call(
        _hgrn_kernel,
        grid=(BATCH, HG_HEADS),
        in_specs=[spec, spec, spec, spec, _resident((1, HG_HEAD_DIM))],
        out_specs=spec,
        out_shape=jax.ShapeDtypeStruct((BATCH, SEQ, MAIN_WIDTH), BF16),
        scratch_shapes=[pltpu.VMEM((HG_SUB + HG_CHUNK, HG_HEAD_DIM), F32),
                        pltpu.VMEM((HG_SUB + HG_CHUNK, HG_HEAD_DIM), F32),
                        pltpu.VMEM((HG_SUB + HG_CHUNK, HG_HEAD_DIM), F32)],
        compiler_params=_params(("parallel", "parallel")),
        name="hgrn",
    )(q, lf, v, g, o_gain)


FOX_BIAS_PIECES = 3
LOG2E = 1.4426950408889634


def _bias_slot_layout():
    lane = lax.broadcasted_iota(jnp.int32, (V7X_LANES, MAIN_WIDTH), 1)
    pair = lane // (2 * FOX_HEAD_DIM)
    half = (lane // FOX_HEAD_DIM) % 2
    head_here = 2 * pair + (1 - half)
    slot = lane % FOX_HEAD_DIM
    return head_here, slot


def _place_bias(pieces, first_slot):
    head_here, slot = _bias_slot_layout()
    head_row = lax.broadcasted_iota(jnp.int32, (V7X_LANES, MAIN_WIDTH), 0)
    place = jnp.concatenate(
        [((head_row == head_here) & (slot == first_slot + i)).astype(BF16)
         for i in range(len(pieces))], axis=0)
    return jnp.dot(jnp.concatenate(pieces, axis=1), place, preferred_element_type=F32)


def _with_bias_lanes(pair_vals, pair_bias, head_in_pair):
    own = (lax.broadcasted_iota(jnp.int32, pair_vals.shape, 1) // FOX_HEAD_DIM) == head_in_pair
    return jnp.where(own, pair_vals, pair_bias)


def _kv_kernel(x_ref, gain_ref, wk_ref, wv_ref, wf_ref, fb_ref, kg_ref,
               ka_ref, vt_ref, qbias_ref, carry_ref, v_ref):
    @pl.when(pl.program_id(1) == 0)
    def _():
        carry_ref[...] = jnp.zeros_like(carry_ref)

    n = _rms(x_ref[0], gain_ref[...]).astype(BF16)

    z = jnp.dot(n, wf_ref[...], preferred_element_type=F32) + fb_ref[...]
    log_f = jnp.minimum(z, 0.0) - jnp.log1p(jnp.exp(-jnp.abs(z)))
    r_i = lax.broadcasted_iota(jnp.int32, (KV_TILE, KV_TILE), 0)
    c_i = lax.broadcasted_iota(jnp.int32, (KV_TILE, KV_TILE), 1)
    tril = (r_i >= c_i).astype(BF16)
    cum = None
    rem = log_f
    for p in range(3):
        piece = rem.astype(BF16)
        term = jnp.dot(tril, piece, preferred_element_type=F32)
        cum = term if cum is None else cum + term
        rem = rem - piece.astype(F32)
    cum = cum + carry_ref[...]
    carry_ref[...] = cum[KV_TILE - 1:KV_TILE, :]

    pieces = []
    rem = cum * LOG2E
    for p in range(FOX_BIAS_PIECES):
        pieces.append(rem.astype(BF16))
        rem = rem - pieces[-1].astype(F32)
    _, slot = _bias_slot_layout()
    slot_row = slot[0:1, :]
    n_p = FOX_BIAS_PIECES
    k_bias = (_place_bias(pieces, 0)
              + ((slot_row >= n_p) & (slot_row < 2 * n_p)).astype(F32)).astype(BF16)
    qbias_ref[0] = (_place_bias(pieces, n_p) - (slot_row < n_p).astype(F32)).astype(BF16)

    pair_w = 2 * FOX_HEAD_DIM
    for c in range(MAIN_WIDTH // 256):
        sl = slice(c * 256, (c + 1) * 256)
        k_raw = jnp.dot(n, wk_ref[:, sl], preferred_element_type=F32)
        k_n = _group_rms(k_raw, kg_ref[:, sl], FOX_HEAD_DIM).astype(BF16)
        for pp in range(256 // pair_w):
            lo = c * 256 + pp * pair_w
            for hh in range(2):
                dst = 2 * lo + hh * pair_w
                ka_ref[0, :, dst:dst + pair_w] = _with_bias_lanes(
                    k_n[:, pp * pair_w:(pp + 1) * pair_w], k_bias[:, lo:lo + pair_w], hh)
    v_ref[...] = jnp.dot(n, wv_ref[...], preferred_element_type=F32)
    vt_ref[0] = v_ref[...].T.astype(BF16)


def _kv(x, gain, wk, wv, wf, f_bias, k_gain_tiled):
    row = lambda w: pl.BlockSpec((1, KV_TILE, w), lambda b, i: (b, i, 0))
    return pl.pallas_call(
        _kv_kernel,
        grid=(BATCH, SEQ // KV_TILE),
        in_specs=[row(D_MODEL), _resident((1, D_MODEL)), _resident((D_MODEL, MAIN_WIDTH)),
                  _resident((D_MODEL, MAIN_WIDTH)), _resident((D_MODEL, V7X_LANES)),
                  _resident((1, V7X_LANES)), _resident((1, MAIN_WIDTH))],
        out_specs=[row(2 * MAIN_WIDTH),
                   pl.BlockSpec((1, MAIN_WIDTH, KV_TILE), lambda b, i: (b, 0, i)),
                   row(MAIN_WIDTH)],
        out_shape=[jax.ShapeDtypeStruct((BATCH, SEQ, 2 * MAIN_WIDTH), BF16),
                   jax.ShapeDtypeStruct((BATCH, MAIN_WIDTH, SEQ), BF16),
                   jax.ShapeDtypeStruct((BATCH, SEQ, MAIN_WIDTH), BF16)],
        scratch_shapes=[pltpu.VMEM((1, V7X_LANES), F32),
                        pltpu.VMEM((KV_TILE, MAIN_WIDTH), F32)],
        compiler_params=_params(("parallel", "arbitrary")),
        name="kv",
    )(x, gain, wk, wv, wf, f_bias, k_gain_tiled)


def _fox_kernel(qa_ref, ka_ref, vt_ref, gate_ref, o_ref, m_ref, acc_ref):
    T, HD = FOX_TILE, FOX_HEAD_DIM
    W = 2 * HD
    qi = pl.program_id(2)
    m_ref[...] = jnp.full(m_ref.shape, -jnp.inf, F32)
    acc_ref[...] = jnp.zeros_like(acc_ref)
    ones_rows = jnp.ones((FOX_ONES_ROWS, T), BF16)
    visible = (lax.broadcasted_iota(jnp.int32, (T, T), 0)
               <= lax.broadcasted_iota(jnp.int32, (T, T), 1))

    def block(j, masked):
        c0 = pl.multiple_of(j * T, T)
        scores = [lax.dot_general(ka_ref[0, pl.ds(c0, T), hh * W:(hh + 1) * W],
                                  qa_ref[0, :, hh * W:(hh + 1) * W],
                                  (((1,), (1,)), ((), ())), preferred_element_type=F32)
                  for hh in range(2)]
        probs, alphas = [], []
        for hh in range(2):
            s = scores[hh]
            if masked:
                s = jnp.where(visible, s, -jnp.inf)
            m_old = m_ref[hh]
            m_new = jnp.maximum(m_old, jnp.max(s, axis=0, keepdims=True))
            probs.append(jnp.exp2(s - m_new).astype(BF16))
            alphas.append(jnp.exp2(m_old - m_new))
            m_ref[hh] = m_new
        for hh in range(2):
            vt = jnp.concatenate([vt_ref[0, hh * HD:(hh + 1) * HD, pl.ds(c0, T)], ones_rows],
                                 axis=0)
            acc_ref[hh] = (alphas[hh] * acc_ref[hh]
                           + jnp.dot(vt, probs[hh], preferred_element_type=F32))

    def loop_body(j, carry):
        block(j, False)
        return carry

    lax.fori_loop(0, qi, loop_body, 0)
    block(qi, True)
    out_t = jnp.concatenate([acc_ref[hh, 0:HD, :] / acc_ref[hh, HD:HD + 1, :]
                             for hh in range(2)], axis=0)
    o_ref[0] = (out_t.T * gate_ref[0]).astype(BF16)


def _fox(qa, ka, vt, gate):
    T = FOX_TILE
    W = 2 * FOX_HEAD_DIM
    return pl.pallas_call(
        _fox_kernel,
        grid=(BATCH, FOX_HEADS // 2, SEQ // T),
        in_specs=[pl.BlockSpec((1, T, 2 * W), lambda b, p, i: (b, i, p)),
                  pl.BlockSpec((1, SEQ, 2 * W), lambda b, p, i: (b, 0, p)),
                  pl.BlockSpec((1, W, SEQ), lambda b, p, i: (b, p, 0)),
                  pl.BlockSpec((1, T, W), lambda b, p, i: (b, i, p))],
        out_specs=pl.BlockSpec((1, T, W), lambda b, p, i: (b, i, p)),
        out_shape=jax.ShapeDtypeStruct((BATCH, SEQ, MAIN_WIDTH), BF16),
        scratch_shapes=[pltpu.VMEM((2, 1, T), F32),
                        pltpu.VMEM((2, FOX_HEAD_DIM + FOX_ONES_ROWS, T), F32)],
        compiler_params=_params(("parallel", "parallel", "parallel")),
        name="fox",
    )(qa, ka, vt, gate)


def _memattn_kernel(qm_ref, mem_ref, mgain_ref, wkv_ref, qg_ref, kg_ref, o_ref,
                    km_ref, vm_ref):
    W, HD = MEM_WIDTH, MEM_HEAD_DIM

    @pl.when(pl.program_id(1) == 0)
    def _():
        mem_n = _rms(mem_ref[0], mgain_ref[...]).astype(BF16)
        kv = jnp.dot(mem_n, wkv_ref[...], preferred_element_type=F32)
        km_ref[...] = _group_rms(kv[:, :W], kg_ref[...], HD).astype(BF16)
        vm_ref[...] = kv[:, W:].astype(BF16)

    scale = HD ** -0.5
    qn = (_group_rms(qm_ref[0], qg_ref[...], HD) * scale).astype(BF16)
    head_of_lane = lax.broadcasted_iota(jnp.int32, qn.shape, 1) // HD
    km = km_ref[...]
    vm = vm_ref[...]
    out = jnp.zeros(qn.shape, F32)
    for h in range(MEM_HEADS):
        mine = head_of_lane == h
        s = lax.dot_general(jnp.where(mine, qn, jnp.zeros_like(qn)), km,
                            (((1,), (1,)), ((), ())), preferred_element_type=F32)
        p = jnp.exp(s - jnp.max(s, axis=-1, keepdims=True))
        inv = 1.0 / jnp.sum(p, axis=-1, keepdims=True)
        pv = jnp.dot(p.astype(BF16), vm, preferred_element_type=F32)
        out = out + jnp.where(mine, pv * inv, 0.0)
    o_ref[0] = out.astype(BF16)


def _memattn(qm, mem, mem_gain, w_mem_kv, q_gain_tiled, k_gain_tiled):
    T = MEM_Q_TILE
    qspec = pl.BlockSpec((1, T, MEM_WIDTH), lambda b, i: (b, i, 0))
    return pl.pallas_call(
        _memattn_kernel,
        grid=(BATCH, SEQ // T),
        in_specs=[qspec,
                  pl.BlockSpec((1, N_MEM, D_MODEL), lambda b, i: (b, 0, 0)),
                  _resident((1, D_MODEL)), _resident((D_MODEL, 2 * MEM_WIDTH)),
                  _resident((1, MEM_WIDTH)), _resident((1, MEM_WIDTH))],
        out_specs=qspec,
        out_shape=jax.ShapeDtypeStruct((BATCH, SEQ, MEM_WIDTH), BF16),
        scratch_shapes=[pltpu.VMEM((N_MEM, MEM_WIDTH), BF16),
                        pltpu.VMEM((N_MEM, MEM_WIDTH), BF16)],
        compiler_params=_params(("parallel", "arbitrary")),
        name="memattn",
    )(qm, mem, mem_gain, w_mem_kv, q_gain_tiled, k_gain_tiled)


def kernel(x, mem, ffn1_norm, ffn1_w_gate, ffn1_w_up, ffn1_w_down, mix_norm, mem_norm,
           w_mem_kv, mem_q_gain, mem_k_gain, w_in_a, hgrn_lb_logits, hgrn_o_gain,
           w_in_b, fox_q_gain, kv_norm, w_kv, fox_f_bias, fox_k_gain, w_out,
           ffn2_norm, ffn2_w_gate, ffn2_w_up, ffn2_w_down):
    bf = lambda t: t.astype(BF16)
    row = lambda t: t.reshape(1, -1).astype(F32)
    x = x.reshape(TOKENS, D_MODEL).astype(F32)
    mem = mem.astype(F32)
    lb_logits = hgrn_lb_logits.astype(F32)

    for l in range(DEPTH):
        x = _ffn(x, row(ffn1_norm[l]), bf(ffn1_w_gate[l]), bf(ffn1_w_up[l]), bf(ffn1_w_down[l]))
        if l < N_A_LAYERS:
            qs, lf, iv, sg, qm = _inproj_a(l, x, row(mix_norm[l]), bf(w_in_a[l]), lb_logits)
            to3 = lambda t: t.reshape(BATCH, SEQ, MAIN_WIDTH)
            main = _hgrn(to3(qs), to3(lf), to3(iv), to3(sg), row(hgrn_o_gain[l]))
        else:
            j = l - N_A_LAYERS
            qa, gate, qm = _inproj_b(x, row(mix_norm[l]), bf(w_in_b[j]),
                                     row(jnp.tile(fox_q_gain[j], FOX_HEADS)),
                                     q_bias.reshape(TOKENS, MAIN_WIDTH))
            main = _fox(qa.reshape(BATCH, SEQ, 2 * MAIN_WIDTH), k_aug, v_t,
                        gate.reshape(BATCH, SEQ, MAIN_WIDTH))
        memo = _memattn(qm.reshape(BATCH, SEQ, MEM_WIDTH), mem, row(mem_norm[l]),
                        bf(w_mem_kv[l]), row(jnp.tile(mem_q_gain[l], MEM_HEADS)),
                        row(jnp.tile(mem_k_gain[l], MEM_HEADS)))
        x = _mix_ffn(x, main.reshape(TOKENS, MAIN_WIDTH), memo.reshape(TOKENS, MEM_WIDTH),
                     bf(w_out[l, :MAIN_WIDTH]), bf(w_out[l, MAIN_WIDTH:]),
                     row(ffn2_norm[l]), bf(ffn2_w_gate[l]), bf(ffn2_w_up[l]), bf(ffn2_w_down[l]))
        if l == N_A_LAYERS - 1:
            pad = V7X_LANES - FOX_HEADS
            wf = jnp.pad(w_kv[:, 2 * MAIN_WIDTH:], ((0, 0), (0, pad)))
            fb = jnp.pad(fox_f_bias.astype(F32), (0, pad)).reshape(1, -1)
            k_aug, v_t, q_bias = _kv(x.reshape(BATCH, SEQ, D_MODEL), row(kv_norm),
                                     bf(w_kv[:, :MAIN_WIDTH]),
                                     bf(w_kv[:, MAIN_WIDTH:2 * MAIN_WIDTH]), bf(wf), fb,
                                     row(jnp.tile(fox_k_gain, FOX_HEADS)))
    return x.reshape(BATCH, SEQ, D_MODEL)
```

```python
import functools

import jax
import jax.numpy as jnp
from jax import lax
from jax.experimental import pallas as pl
from jax.experimental.pallas import tpu as pltpu

F32 = jnp.float32
BF16 = jnp.bfloat16

D_MODEL = 1024
BATCH = 8
SEQ = 2048
DEPTH = 4
N_MEM = 256
N_A_LAYERS = DEPTH // 2
MAIN_WIDTH = 768
MEM_WIDTH = 256
HG_HEAD_DIM = 128
HG_HEADS = MAIN_WIDTH // HG_HEAD_DIM
FOX_HEAD_DIM = 64
FOX_HEADS = MAIN_WIDTH // FOX_HEAD_DIM
MEM_HEADS = 4
MEM_HEAD_DIM = MEM_WIDTH // MEM_HEADS
D_FF = 2816
EPS = 1e-6
TOKENS = BATCH * SEQ

V7X_LANES = 128
V7X_VMEM_BYTES = 64 * 1024 * 1024

ROW_TILE = 512
FF_CHUNK = 256
HG_CHUNK = 64
HG_SUB = 16
HG_SAFE_DECAY = 60.0
HG_GROUP = 4
FOX_TILE = 512
ONES_ROWS = 16
MEM_Q_TILE = 2048
KV_TILE = 512
VMEM_LIMIT = 56 * 1024 * 1024


def _rms(x, gain):
    ms = jnp.mean(x * x, axis=-1, keepdims=True)
    return x * lax.rsqrt(ms + EPS) * gain


def _split_dot(x, rhs01, parts):
    acc = None
    rem = x
    for p in range(parts):
        piece = rem.astype(BF16)
        term = jnp.dot(piece, rhs01, preferred_element_type=F32)
        acc = term if acc is None else acc + term
        if p + 1 < parts:
            rem = rem - piece.astype(F32)
    return acc


def _group_ones(width, group):
    r = lax.broadcasted_iota(jnp.int32, (width, width), 0) // group
    c = lax.broadcasted_iota(jnp.int32, (width, width), 1) // group
    return (r == c).astype(BF16)


def _group_rms(x, gain_tiled, group):
    width = x.shape[-1]
    ss = _split_dot(x * x, _group_ones(width, group), 2)
    return x * lax.rsqrt(ss * (1.0 / group) + EPS) * gain_tiled


def _resident(shape, index=None):
    index = (0,) * len(shape) if index is None else index
    return pl.BlockSpec(shape, lambda *_: index, pipeline_mode=pl.Buffered(1))


def _layer(shape, layer, first=0):
    index = (layer, first) + (0,) * (len(shape) - 1)
    return pl.BlockSpec((None,) + tuple(shape), lambda *_: index,
                        pipeline_mode=pl.Buffered(1))


def _params(sem):
    return pltpu.CompilerParams(dimension_semantics=sem, vmem_limit_bytes=VMEM_LIMIT)


def _swiglu_rows(x, gain_ref, wg_ref, wu_ref, wd_ref, h_ref):
    n = _rms(x, gain_ref[...]).astype(BF16)
    for c in range(D_FF // FF_CHUNK):
        sl = slice(c * FF_CHUNK, (c + 1) * FF_CHUNK)
        g = jnp.dot(n, wg_ref[:, sl], preferred_element_type=F32)
        u = jnp.dot(n, wu_ref[:, sl], preferred_element_type=F32)
        h_ref[:, sl] = (g * jax.nn.sigmoid(g) * u).astype(BF16)
    y = jnp.dot(h_ref[...], wd_ref[...], preferred_element_type=F32)
    return x + 0.5 * y


def _ffn_kernel(x_ref, gain_ref, wg_ref, wu_ref, wd_ref, o_ref, h_ref):
    o_ref[...] = _swiglu_rows(x_ref[...], gain_ref, wg_ref, wu_ref, wd_ref, h_ref)


def _mix_ffn_kernel(x_ref, main_ref, memo_ref, wo_main_ref, wo_mem_ref,
                    gain_ref, wg_ref, wu_ref, wd_ref, o_ref, h_ref):
    x = (x_ref[...]
         + jnp.dot(main_ref[...], wo_main_ref[...], preferred_element_type=F32)
         + jnp.dot(memo_ref[...], wo_mem_ref[...], preferred_element_type=F32))
    o_ref[...] = _swiglu_rows(x, gain_ref, wg_ref, wu_ref, wd_ref, h_ref)


def _row_spec(width):
    return pl.BlockSpec((ROW_TILE, width), lambda i: (i, 0))


def _ffn_weight_specs(layer):
    return [_layer((1, D_MODEL), layer), _layer((D_MODEL, D_FF), layer),
            _layer((D_MODEL, D_FF), layer), _layer((D_FF, D_MODEL), layer)]


def _ffn(layer, x, gains, wg, wu, wd):
    return pl.pallas_call(
        _ffn_kernel,
        grid=(TOKENS // ROW_TILE,),
        in_specs=[_row_spec(D_MODEL)] + _ffn_weight_specs(layer),
        out_specs=_row_spec(D_MODEL),
        out_shape=jax.ShapeDtypeStruct((TOKENS, D_MODEL), F32),
        scratch_shapes=[pltpu.VMEM((ROW_TILE, D_FF), BF16)],
        compiler_params=_params(("parallel",)),
        name="ffn",
    )(x, gains, wg, wu, wd)


def _mix_ffn(layer, x, main, memo, w_out, gains, wg, wu, wd):
    assert MAIN_WIDTH % MEM_WIDTH == 0
    return pl.pallas_call(
        _mix_ffn_kernel,
        grid=(TOKENS // ROW_TILE,),
        in_specs=[_row_spec(D_MODEL), _row_spec(MAIN_WIDTH), _row_spec(MEM_WIDTH),
                  _layer((MAIN_WIDTH, D_MODEL), layer),
                  _layer((MEM_WIDTH, D_MODEL), layer, MAIN_WIDTH // MEM_WIDTH)]
        + _ffn_weight_specs(layer),
        out_specs=_row_spec(D_MODEL),
        out_shape=jax.ShapeDtypeStruct((TOKENS, D_MODEL), F32),
        scratch_shapes=[pltpu.VMEM((ROW_TILE, D_FF), BF16)],
        compiler_params=_params(("parallel",)),
        name="mix_ffn",
    )(x, main, memo, w_out, w_out, gains, wg, wu, wd)


def _inproj_a_kernel(layer, x_ref, gain_ref, w_ref, lbl_ref,
                     q_ref, lf_ref, i_ref, g_ref, qm_ref):
    n = _rms(x_ref[...], gain_ref[...]).astype(BF16)
    rows = [lbl_ref[r:r + 1, :] for r in range(N_A_LAYERS)]
    top = functools.reduce(jnp.maximum, rows)
    exps = [jnp.exp(r - top) for r in rows]
    total = functools.reduce(jnp.add, exps)
    lb = jnp.zeros_like(top)
    for r in range(1, layer + 1):
        lb = lb + exps[r] / total

    w = MAIN_WIDTH
    q_raw = jnp.dot(n, w_ref[:, 0:w], preferred_element_type=F32)
    q_ref[...] = q_raw * jax.nn.sigmoid(q_raw)
    f_raw = jnp.dot(n, w_ref[:, w:2 * w], preferred_element_type=F32)
    lf_ref[...] = jnp.log(lb + (1.0 - lb) * jax.nn.sigmoid(f_raw))
    i_ref[...] = jnp.dot(n, w_ref[:, 2 * w:3 * w], preferred_element_type=F32).astype(BF16)
    g_raw = jnp.dot(n, w_ref[:, 3 * w:4 * w], preferred_element_type=F32)
    g_ref[...] = g_raw * jax.nn.sigmoid(g_raw)
    qm_ref[...] = jnp.dot(n, w_ref[:, 4 * w:], preferred_element_type=F32)


def _inproj_a(layer, x, gains, w_in, lb_logits):
    a_in = 4 * MAIN_WIDTH + MEM_WIDTH
    return pl.pallas_call(
        functools.partial(_inproj_a_kernel, layer),
        grid=(TOKENS // ROW_TILE,),
        in_specs=[_row_spec(D_MODEL), _layer((1, D_MODEL), layer),
                  _layer((D_MODEL, a_in), layer), _resident((N_A_LAYERS, MAIN_WIDTH))],
        out_specs=[_row_spec(MAIN_WIDTH), _row_spec(MAIN_WIDTH), _row_spec(MAIN_WIDTH),
                   _row_spec(MAIN_WIDTH), _row_spec(MEM_WIDTH)],
        out_shape=[jax.ShapeDtypeStruct((TOKENS, MAIN_WIDTH), F32),
                   jax.ShapeDtypeStruct((TOKENS, MAIN_WIDTH), F32),
                   jax.ShapeDtypeStruct((TOKENS, MAIN_WIDTH), BF16),
                   jax.ShapeDtypeStruct((TOKENS, MAIN_WIDTH), F32),
                   jax.ShapeDtypeStruct((TOKENS, MEM_WIDTH), F32)],
        compiler_params=_params(("parallel",)),
        name="inproj_a",
    )(x, gains, w_in, lb_logits)


def _inproj_b_kernel(x_ref, gain_ref, w_ref, qg_ref, qbias_ref, qa_ref, gate_ref, qm_ref):
    n = _rms(x_ref[...], gain_ref[...]).astype(BF16)
    w = MAIN_WIDTH
    scale = FOX_HEAD_DIM ** -0.5 * LOG2E
    pair_w = 2 * FOX_HEAD_DIM
    for c in range(w // 256):
        sl = slice(c * 256, (c + 1) * 256)
        q_raw = jnp.dot(n, w_ref[:, sl], preferred_element_type=F32)
        q_n = (_group_rms(q_raw, qg_ref[:, sl], FOX_HEAD_DIM) * scale).astype(BF16)
        for pp in range(256 // pair_w):
            lo = c * 256 + pp * pair_w
            for hh in range(2):
                dst = 2 * lo + hh * pair_w
                qa_ref[:, dst:dst + pair_w] = _with_bias_lanes(
                    q_n[:, pp * pair_w:(pp + 1) * pair_w], qbias_ref[:, lo:lo + pair_w], hh)
    gate = jnp.dot(n, w_ref[:, w:2 * w], preferred_element_type=F32)
    gate_ref[...] = jax.nn.sigmoid(gate)
    qm_ref[...] = jnp.dot(n, w_ref[:, 2 * w:], preferred_element_type=F32)


def _inproj_b(layer, x, gains, w_in, q_gain_tiled, q_bias):
    b_in = 2 * MAIN_WIDTH + MEM_WIDTH
    j = layer - N_A_LAYERS
    return pl.pallas_call(
        _inproj_b_kernel,
        grid=(TOKENS // ROW_TILE,),
        in_specs=[_row_spec(D_MODEL), _layer((1, D_MODEL), layer),
                  _layer((D_MODEL, b_in), j), _layer((1, MAIN_WIDTH), j),
                  _row_spec(MAIN_WIDTH)],
        out_specs=[_row_spec(2 * MAIN_WIDTH), _row_spec(MAIN_WIDTH), _row_spec(MEM_WIDTH)],
        out_shape=[jax.ShapeDtypeStruct((TOKENS, 2 * MAIN_WIDTH), BF16),
                   jax.ShapeDtypeStruct((TOKENS, MAIN_WIDTH), F32),
                   jax.ShapeDtypeStruct((TOKENS, MEM_WIDTH), F32)],
        compiler_params=_params(("parallel",)),
        name="inproj_b",
    )(x, gains, w_in, q_gain_tiled, q_bias)


def _hgrn_kernel(q_ref, lf_ref, v_ref, g_ref, gain_ref, o_ref,
                 cpad_ref, kpad_ref, vpad_ref):
    C, SB, HD = HG_CHUNK, HG_SUB, HG_HEAD_DIM
    n_sub = C // SB
    r_i = lax.broadcasted_iota(jnp.int32, (C, C), 0)
    c_i = lax.broadcasted_iota(jnp.int32, (C, C), 1)
    tril = (r_i >= c_i).astype(BF16)
    gain = gain_ref[...]

    def intra_exact(q, k, v16, c):
        outs = [jnp.zeros((SB, HD), F32)]
        for b in range(1, n_sub):
            lo = b * SB
            ref = c[lo - 1:lo, :]
            qb = (q[lo:lo + SB, :] * jnp.exp(c[lo:lo + SB, :] - ref)).astype(BF16)
            kb = (k[0:lo, :] * jnp.exp(ref - c[0:lo, :])).astype(BF16)
            p_b = lax.dot_general(qb, kb, (((1,), (1,)), ((), ())),
                                  preferred_element_type=F32)
            outs.append(jnp.dot(p_b.astype(BF16), v16[0:lo, :], preferred_element_type=F32))
        o = jnp.concatenate(outs, axis=0)
        ones = jnp.ones((HD, HD), BF16)
        row_in_block = lax.broadcasted_iota(jnp.int32, (C, HD), 0) % SB
        cpad_ref[SB:SB + C, :] = c
        kpad_ref[SB:SB + C, :] = k
        vpad_ref[SB:SB + C, :] = v16.astype(F32)
        for d in range(SB):
            cs = cpad_ref[SB - d:SB - d + C, :]
            ks = kpad_ref[SB - d:SB - d + C, :]
            vs = vpad_ref[SB - d:SB - d + C, :]
            wgt = jnp.where(row_in_block >= d, q * ks * jnp.exp(c - cs), 0.0)
            s = jnp.dot(wgt.astype(BF16), ones, preferred_element_type=F32)
            o = o + s * vs
        return o

    def chunk(n, st, intra):
        r0 = pl.multiple_of(n * C, C)
        lf = lf_ref[0, pl.ds(r0, C), :]
        q = q_ref[0, pl.ds(r0, C), :]
        v16 = v_ref[0, pl.ds(r0, C), :]
        k = 1.0 - jnp.exp(lf)
        c = None
        rem = lf
        for p in range(3):
            piece = rem.astype(BF16)
            term = jnp.dot(tril, piece, preferred_element_type=F32)
            c = term if c is None else c + term
            rem = rem - piece.astype(F32)
        c_end = c[C - 1:C, :]

        qe = (q * jnp.exp(c)).astype(BF16)
        o = lax.dot_general(qe, st.astype(BF16), (((1,), (1,)), ((), ())),
                            preferred_element_type=F32)
        kd = (k * jnp.exp(c_end - c)).astype(BF16)
        upd = lax.dot_general(v16, kd, (((0,), (0,)), ((), ())),
                              preferred_element_type=F32)
        o = o + intra(q, k, v16, c)
        o_ref[0, pl.ds(r0, C), :] = (_rms(o, gain) * g_ref[0, pl.ds(r0, C), :]).astype(BF16)
        return st * jnp.exp(c_end) + upd

    G = HG_GROUP
    R = G * C
    gr = lax.broadcasted_iota(jnp.int32, (R, R), 0)
    gc = lax.broadcasted_iota(jnp.int32, (R, R), 1)
    tril_group = ((gr // C == gc // C) & (gr >= gc)).astype(BF16)
    sub_of_row = lax.broadcasted_iota(jnp.int32, (C, HD), 0) // SB
    causal = r_i >= c_i

    def group(i, st):
        r0 = pl.multiple_of(i * R, R)
        lf = lf_ref[0, pl.ds(r0, R), :]
        q = q_ref[0, pl.ds(r0, R), :]
        v16 = v_ref[0, pl.ds(r0, R), :]
        k = 1.0 - jnp.exp(lf)
        c = None
        rem = lf
        for p in range(3):
            piece = rem.astype(BF16)
            term = jnp.dot(tril_group, piece, preferred_element_type=F32)
            c = term if c is None else c + term
            rem = rem - piece.astype(F32)

        lhs, rhs, c_ends = [], [], []
        for g in range(G):
            cg = c[g * C:(g + 1) * C, :]
            qg = q[g * C:(g + 1) * C, :]
            kg = k[g * C:(g + 1) * C, :]
            refs = [jnp.zeros((1, HD), F32)] + [cg[b * SB - 1:b * SB, :] for b in range(1, n_sub)]
            own_ref = jnp.concatenate([jnp.broadcast_to(r, (SB, HD)) for r in refs], axis=0)
            k_own = kg * jnp.exp(own_ref - cg)
            lhs.append(jnp.concatenate(
                [(qg * jnp.exp(jnp.minimum(cg - r, 0.0))).astype(BF16) for r in refs], axis=1))
            rhs.append(jnp.concatenate(
                [jnp.where(sub_of_row == b, k_own, 0.0).astype(BF16) for b in range(n_sub)],
                axis=1))
            c_ends.append(cg[C - 1:C, :])
        c_end_rows = jnp.concatenate([jnp.broadcast_to(e, (C, HD)) for e in c_ends], axis=0)
        qe = (q * jnp.exp(c)).astype(BF16)
        kd = (k * jnp.exp(c_end_rows - c)).astype(BF16)

        scores = [lax.dot_general(lhs[g], rhs[g], (((1,), (1,)), ((), ())),
                                  preferred_element_type=F32) for g in range(G)]
        upds = [lax.dot_general(v16[g * C:(g + 1) * C, :], kd[g * C:(g + 1) * C, :],
                                (((0,), (0,)), ((), ())), preferred_element_type=F32)
                for g in range(G)]
        intra = [jnp.dot(jnp.where(causal, scores[g], 0.0).astype(BF16),
                         v16[g * C:(g + 1) * C, :], preferred_element_type=F32)
                 for g in range(G)]
        outs = []
        for g in range(G):
            inter = lax.dot_general(qe[g * C:(g + 1) * C, :], st.astype(BF16),
                                    (((1,), (1,)), ((), ())), preferred_element_type=F32)
            outs.append(inter + intra[g])
            st = st * jnp.exp(c_ends[g]) + upds[g]
        o = jnp.concatenate(outs, axis=0)
        o_ref[0, pl.ds(r0, R), :] = (_rms(o, gain) * g_ref[0, pl.ds(r0, R), :]).astype(BF16)
        return st

    def run_exact():
        def body(n, st):
            return chunk(n, st, intra_exact)
        lax.fori_loop(0, SEQ // C, body, jnp.zeros((HD, HD), F32))

    blk = lax.broadcasted_iota(jnp.int32, (SEQ // SB, SEQ), 0)
    tok = lax.broadcasted_iota(jnp.int32, (SEQ // SB, SEQ), 1) // SB
    block_decay = jnp.dot((blk == tok).astype(BF16), lf_ref[0].astype(BF16),
                          preferred_element_type=F32)
    mild = jnp.min(block_decay) >= -HG_SAFE_DECAY

    @pl.when(mild)
    def _():
        lax.fori_loop(0, SEQ // R, group, jnp.zeros((HD, HD), F32))

    @pl.when(jnp.logical_not(mild))
    def _():
        zpad = jnp.zeros((SB, HD), F32)
        cpad_ref[0:SB, :] = zpad
        kpad_ref[0:SB, :] = zpad
        vpad_ref[0:SB, :] = zpad
        run_exact()


def _hgrn(layer, q, lf, v, g, o_gains):
    spec = pl.BlockSpec((1, SEQ, HG_HEAD_DIM), lambda b, h: (b, 0, h))
    return pl.pallas_call(
        _hgrn_kernel,
        grid=(BATCH, HG_HEADS),
        in_specs=[spec, spec, spec, spec, _layer((1, HG_HEAD_DIM), layer)],
        out_specs=spec,
        out_shape=jax.ShapeDtypeStruct((BATCH, SEQ, MAIN_WIDTH), BF16),
        scratch_shapes=[pltpu.VMEM((HG_SUB + HG_CHUNK, HG_HEAD_DIM), F32),
                        pltpu.VMEM((HG_SUB + HG_CHUNK, HG_HEAD_DIM), F32),
                        pltpu.VMEM((HG_SUB + HG_CHUNK, HG_HEAD_DIM), F32)],
        compiler_params=_params(("parallel", "parallel")),
        name="hgrn",
    )(q, lf, v, g, o_gains)


FOX_BIAS_PIECES = 3
LOG2E = 1.4426950408889634


def _bias_slot_layout():
    lane = lax.broadcasted_iota(jnp.int32, (V7X_LANES, MAIN_WIDTH), 1)
    pair = lane // (2 * FOX_HEAD_DIM)
    half = (lane // FOX_HEAD_DIM) % 2
    head_here = 2 * pair + (1 - half)
    slot = lane % FOX_HEAD_DIM
    return head_here, slot


def _place_bias(pieces, first_slot):
    head_here, slot = _bias_slot_layout()
    head_row = lax.broadcasted_iota(jnp.int32, (V7X_LANES, MAIN_WIDTH), 0)
    place = jnp.concatenate(
        [((head_row == head_here) & (slot == first_slot + i)).astype(BF16)
         for i in range(len(pieces))], axis=0)
    return jnp.dot(jnp.concatenate(pieces, axis=1), place, preferred_element_type=F32)


def _with_bias_lanes(pair_vals, pair_bias, head_in_pair):
    own = (lax.broadcasted_iota(jnp.int32, pair_vals.shape, 1) // FOX_HEAD_DIM) == head_in_pair
    return jnp.where(own, pair_vals, pair_bias)


def _kv_kernel(x_ref, gain_ref, wk_ref, wv_ref, wf_ref, fb_ref, kg_ref,
               ka_ref, vt_ref, qbias_ref, carry_ref, v_ref):
    @pl.when(pl.program_id(1) == 0)
    def _():
        carry_ref[...] = jnp.zeros_like(carry_ref)

    n = _rms(x_ref[0], gain_ref[...]).astype(BF16)

    z = jnp.dot(n, wf_ref[...], preferred_element_type=F32) + fb_ref[...]
    log_f = jnp.minimum(z, 0.0) - jnp.log1p(jnp.exp(-jnp.abs(z)))
    r_i = lax.broadcasted_iota(jnp.int32, (KV_TILE, KV_TILE), 0)
    c_i = lax.broadcasted_iota(jnp.int32, (KV_TILE, KV_TILE), 1)
    tril = (r_i >= c_i).astype(BF16)
    cum = None
    rem = log_f
    for p in range(3):
        piece = rem.astype(BF16)
        term = jnp.dot(tril, piece, preferred_element_type=F32)
        cum = term if cum is None else cum + term
        rem = rem - piece.astype(F32)
    cum = cum + carry_ref[...]
    carry_ref[...] = cum[KV_TILE - 1:KV_TILE, :]

    pieces = []
    rem = cum * LOG2E
    for p in range(FOX_BIAS_PIECES):
        pieces.append(rem.astype(BF16))
        rem = rem - pieces[-1].astype(F32)
    _, slot = _bias_slot_layout()
    slot_row = slot[0:1, :]
    n_p = FOX_BIAS_PIECES
    k_bias = (_place_bias(pieces, 0)
              + ((slot_row >= n_p) & (slot_row < 2 * n_p)).astype(F32)).astype(BF16)
    qbias_ref[0] = (_place_bias(pieces, n_p) - (slot_row < n_p).astype(F32)).astype(BF16)

    pair_w = 2 * FOX_HEAD_DIM
    for c in range(MAIN_WIDTH // 256):
        sl = slice(c * 256, (c + 1) * 256)
        k_raw = jnp.dot(n, wk_ref[:, sl], preferred_element_type=F32)
        k_n = _group_rms(k_raw, kg_ref[:, sl], FOX_HEAD_DIM).astype(BF16)
        for pp in range(256 // pair_w):
            lo = c * 256 + pp * pair_w
            for hh in range(2):
                dst = 2 * lo + hh * pair_w
                ka_ref[0, :, dst:dst + pair_w] = _with_bias_lanes(
                    k_n[:, pp * pair_w:(pp + 1) * pair_w], k_bias[:, lo:lo + pair_w], hh)
    v_ref[...] = jnp.dot(n, wv_ref[...], preferred_element_type=F32)
    vt_ref[0] = v_ref[...].T.astype(BF16)


def _kv(x, gain, w_kv, wf, f_bias, k_gain_tiled):
    row = lambda w: pl.BlockSpec((1, KV_TILE, w), lambda b, i: (b, i, 0))
    return pl.pallas_call(
        _kv_kernel,
        grid=(BATCH, SEQ // KV_TILE),
        in_specs=[row(D_MODEL), _resident((1, D_MODEL)),
                  _resident((D_MODEL, MAIN_WIDTH), (0, 0)),
                  _resident((D_MODEL, MAIN_WIDTH), (0, 1)), _resident((D_MODEL, V7X_LANES)),
                  _resident((1, V7X_LANES)), _resident((1, MAIN_WIDTH))],
        out_specs=[row(2 * MAIN_WIDTH),
                   pl.BlockSpec((1, MAIN_WIDTH, KV_TILE), lambda b, i: (b, 0, i)),
                   row(MAIN_WIDTH)],
        out_shape=[jax.ShapeDtypeStruct((BATCH, SEQ, 2 * MAIN_WIDTH), BF16),
                   jax.ShapeDtypeStruct((BATCH, MAIN_WIDTH, SEQ), BF16),
                   jax.ShapeDtypeStruct((BATCH, SEQ, MAIN_WIDTH), BF16)],
        scratch_shapes=[pltpu.VMEM((1, V7X_LANES), F32),
                        pltpu.VMEM((KV_TILE, MAIN_WIDTH), F32)],
        compiler_params=_params(("parallel", "arbitrary")),
        name="kv",
    )(x, gain, w_kv, w_kv, wf, f_bias, k_gain_tiled)


def _fox_kernel(qa_ref, ka_ref, vt_ref, gate_ref, o_ref, s_ref, m_ref, acc_ref):
    T, HD = FOX_TILE, FOX_HEAD_DIM
    W = 2 * HD
    ones_rows = jnp.ones((ONES_ROWS, T), BF16)
    visible = (lax.broadcasted_iota(jnp.int32, (T, T), 0)
               <= lax.broadcasted_iota(jnp.int32, (T, T), 1))

    def issue_scores(slot, i, j):
        for hh in range(2):
            s_ref[slot, hh] = lax.dot_general(
                ka_ref[0, j * T:(j + 1) * T, hh * W:(hh + 1) * W],
                qa_ref[0, i * T:(i + 1) * T, hh * W:(hh + 1) * W],
                (((1,), (1,)), ((), ())), preferred_element_type=F32)

    def consume(slot, i, j):
        probs, alphas = [], []
        for hh in range(2):
            s = s_ref[slot, hh]
            if j == i:
                s = jnp.where(visible, s, -jnp.inf)
            m_new = jnp.max(s, axis=0, keepdims=True)
            if j:
                m_old = m_ref[hh]
                m_new = jnp.maximum(m_old, m_new)
                alphas.append(jnp.exp2(m_old - m_new))
            probs.append(jnp.exp2(s - m_new).astype(BF16))
            m_ref[hh] = m_new
        for hh in range(2):
            vt = jnp.concatenate([vt_ref[0, hh * HD:(hh + 1) * HD, j * T:(j + 1) * T],
                                 ones_rows], axis=0)
            pv = jnp.dot(vt, probs[hh], preferred_element_type=F32)
            acc_ref[hh] = alphas[hh] * acc_ref[hh] + pv if j else pv
        if j == i:
            out_t = jnp.concatenate([acc_ref[hh, 0:HD, :] / acc_ref[hh, HD:HD + 1, :]
                                     for hh in range(2)], axis=0)
            rows = slice(i * T, (i + 1) * T)
            o_ref[0, rows, :] = (out_t.T * gate_ref[0, rows, :]).astype(BF16)

    blocks = [(i, j) for i in range(SEQ // T) for j in range(i + 1)]
    issue_scores(0, *blocks[0])
    for n, (i, j) in enumerate(blocks):
        if n + 1 < len(blocks):
            issue_scores((n + 1) % 2, *blocks[n + 1])
        consume(n % 2, i, j)


def _fox(qa, ka, vt, gate):
    T = FOX_TILE
    W = 2 * FOX_HEAD_DIM
    seq_spec = lambda w: pl.BlockSpec((1, SEQ, w), lambda b, p: (b, 0, p))
    return pl.pallas_call(
        _fox_kernel,
        grid=(BATCH, FOX_HEADS // 2),
        in_specs=[seq_spec(2 * W), seq_spec(2 * W),
                  pl.BlockSpec((1, W, SEQ), lambda b, p: (b, p, 0)),
                  seq_spec(W)],
        out_specs=seq_spec(W),
        out_shape=jax.ShapeDtypeStruct((BATCH, SEQ, MAIN_WIDTH), BF16),
        scratch_shapes=[pltpu.VMEM((2, 2, T, T), F32),
                        pltpu.VMEM((2, 1, T), F32),
                        pltpu.VMEM((2, FOX_HEAD_DIM + ONES_ROWS, T), F32)],
        compiler_params=_params(("parallel", "parallel")),
        name="fox",
    )(qa, ka, vt, gate)


def _memattn_kernel(qm_ref, mem_ref, mgain_ref, wkv_ref, qg_ref, kg_ref, o_ref,
                    kv_ref, km_ref, vmt_ref):
    W, HD = MEM_WIDTH, MEM_HEAD_DIM

    @pl.when(pl.program_id(1) == 0)
    def _():
        mem_n = _rms(mem_ref[0], mgain_ref[...]).astype(BF16)
        kv_ref[...] = jnp.dot(mem_n, wkv_ref[...], preferred_element_type=F32)
        km = _group_rms(kv_ref[:, :W], kg_ref[...], HD).astype(BF16)
        head_of_lane = lax.broadcasted_iota(jnp.int32, km.shape, 1) // HD
        vm_t = kv_ref[:, W:].T.astype(BF16)
        ones_rows = jnp.ones((ONES_ROWS, N_MEM), BF16)
        for h in range(MEM_HEADS):
            km_ref[h] = jnp.where(head_of_lane == h, km, jnp.zeros_like(km))
            vmt_ref[h] = jnp.concatenate([vm_t[h * HD:(h + 1) * HD, :], ones_rows], axis=0)

    scale = HD ** -0.5 * LOG2E
    qn = (_group_rms(qm_ref[0], qg_ref[...], HD) * scale).astype(BF16)
    scores = [lax.dot_general(km_ref[h], qn, (((1,), (1,)), ((), ())),
                              preferred_element_type=F32) for h in range(MEM_HEADS)]
    outs = []
    for h in range(MEM_HEADS):
        s = scores[h]
        p = jnp.exp2(s - jnp.max(s, axis=0, keepdims=True)).astype(BF16)
        r = jnp.dot(vmt_ref[h], p, preferred_element_type=F32)
        outs.append(r[0:HD, :] / r[HD:HD + 1, :])
    o_ref[0] = jnp.concatenate(outs, axis=0).T.astype(BF16)


def _memattn(layer, qm, mem, mem_gains, w_mem_kv, q_gains_tiled, k_gains_tiled):
    T = MEM_Q_TILE
    qspec = pl.BlockSpec((1, T, MEM_WIDTH), lambda b, i: (b, i, 0))
    return pl.pallas_call(
        _memattn_kernel,
        grid=(BATCH, SEQ // T),
        in_specs=[qspec,
                  pl.BlockSpec((1, N_MEM, D_MODEL), lambda b, i: (b, 0, 0)),
                  _layer((1, D_MODEL), layer), _layer((D_MODEL, 2 * MEM_WIDTH), layer),
                  _layer((1, MEM_WIDTH), layer), _layer((1, MEM_WIDTH), layer)],
        out_specs=qspec,
        out_shape=jax.ShapeDtypeStruct((BATCH, SEQ, MEM_WIDTH), BF16),
        scratch_shapes=[pltpu.VMEM((N_MEM, 2 * MEM_WIDTH), F32),
                        pltpu.VMEM((MEM_HEADS, N_MEM, MEM_WIDTH), BF16),
                        pltpu.VMEM((MEM_HEADS, MEM_HEAD_DIM + ONES_ROWS, N_MEM), BF16)],
        compiler_params=_params(("parallel", "arbitrary")),
        name="memattn",
    )(qm, mem, mem_gains, w_mem_kv, q_gains_tiled, k_gains_tiled)


def kernel(x, mem, ffn1_norm, ffn1_w_gate, ffn1_w_up, ffn1_w_down, mix_norm, mem_norm,
           w_mem_kv, mem_q_gain, mem_k_gain, w_in_a, hgrn_lb_logits, hgrn_o_gain,
           w_in_b, fox_q_gain, kv_norm, w_kv, fox_f_bias, fox_k_gain, w_out,
           ffn2_norm, ffn2_w_gate, ffn2_w_up, ffn2_w_down):
    bf = lambda t: t.astype(BF16)
    row = lambda t: t.reshape(1, -1).astype(F32)
    rows = lambda t, reps=1: jnp.tile(t.astype(F32), (1, reps)).reshape(t.shape[0], 1, -1)
    x = x.reshape(TOKENS, D_MODEL).astype(F32)
    mem = mem.astype(F32)
    lb_logits = hgrn_lb_logits.astype(F32)
    ffn1 = (rows(ffn1_norm), bf(ffn1_w_gate), bf(ffn1_w_up), bf(ffn1_w_down))
    ffn2 = (rows(ffn2_norm), bf(ffn2_w_gate), bf(ffn2_w_up), bf(ffn2_w_down))
    mix_gains, w_a, w_b, w_o = rows(mix_norm), bf(w_in_a), bf(w_in_b), bf(w_out)
    o_gains, fox_q_gains = rows(hgrn_o_gain), rows(fox_q_gain, FOX_HEADS)
    mem_args = (rows(mem_norm), bf(w_mem_kv), rows(mem_q_gain, MEM_HEADS),
                rows(mem_k_gain, MEM_HEADS))

    for l in range(DEPTH):
        x = _ffn(l, x, *ffn1)
        if l < N_A_LAYERS:
            qs, lf, iv, sg, qm = _inproj_a(l, x, mix_gains, w_a, lb_logits)
            to3 = lambda t: t.reshape(BATCH, SEQ, MAIN_WIDTH)
            main = _hgrn(l, to3(qs), to3(lf), to3(iv), to3(sg), o_gains)
        else:
            qa, gate, qm = _inproj_b(l, x, mix_gains, w_b, fox_q_gains,
                                     q_bias.reshape(TOKENS, MAIN_WIDTH))
            main = _fox(qa.reshape(BATCH, SEQ, 2 * MAIN_WIDTH), k_aug, v_t,
                        gate.reshape(BATCH, SEQ, MAIN_WIDTH))
        memo = _memattn(l, qm.reshape(BATCH, SEQ, MEM_WIDTH), mem, *mem_args)
        x = _mix_ffn(l, x, main.reshape(TOKENS, MAIN_WIDTH), memo.reshape(TOKENS, MEM_WIDTH),
                     w_o, *ffn2)
        if l == N_A_LAYERS - 1:
            pad = V7X_LANES - FOX_HEADS
            wf = jnp.pad(w_kv[:, 2 * MAIN_WIDTH:], ((0, 0), (0, pad)))
            fb = jnp.pad(fox_f_bias.astype(F32), (0, pad)).reshape(1, -1)
            k_aug, v_t, q_bias = _kv(x.reshape(BATCH, SEQ, D_MODEL), row(kv_norm),
                                     bf(w_kv), bf(wf), fb,
                                     row(jnp.tile(fox_k_gain, FOX_HEADS)))
    return x.reshape(BATCH, SEQ, D_MODEL)
```

```python
import functools

import jax
import jax.numpy as jnp
from jax import lax
from jax.experimental import pallas as pl
from jax.experimental.pallas import tpu as pltpu

F32 = jnp.float32
BF16 = jnp.bfloat16

D_MODEL = 1024
BATCH = 8
SEQ = 2048
DEPTH = 4
N_MEM = 256
N_A_LAYERS = DEPTH // 2
MAIN_WIDTH = 768
MEM_WIDTH = 256
HG_HEAD_DIM = 128
HG_HEADS = MAIN_WIDTH // HG_HEAD_DIM
FOX_HEAD_DIM = 64
FOX_HEADS = MAIN_WIDTH // FOX_HEAD_DIM
MEM_HEADS = 4
MEM_HEAD_DIM = MEM_WIDTH // MEM_HEADS
D_FF = 2816
EPS = 1e-6
TOKENS = BATCH * SEQ

V7X_LANES = 128
V7X_VMEM_BYTES = 64 * 1024 * 1024

ROW_TILE = 1024
FF_CHUNK = 256
HG_CHUNK = 64
HG_SUB = 16
HG_SAFE_DECAY = 60.0
HG_GROUP = 4
FOX_TILE = 512
ONES_ROWS = 16
MEM_Q_TILE = 2048
KV_TILE = 512
KV_CUM_SEG = 256
VMEM_LIMIT = 56 * 1024 * 1024


def _rms(x, gain):
    ms = jnp.mean(x * x, axis=-1, keepdims=True)
    return x * lax.rsqrt(ms + EPS) * gain


def _split_dot(x, rhs01, parts):
    acc = None
    rem = x
    for p in range(parts):
        piece = rem.astype(BF16)
        term = jnp.dot(piece, rhs01, preferred_element_type=F32)
        acc = term if acc is None else acc + term
        if p + 1 < parts:
            rem = rem - piece.astype(F32)
    return acc


def _group_ones(width, group):
    r = lax.broadcasted_iota(jnp.int32, (width, width), 0) // group
    c = lax.broadcasted_iota(jnp.int32, (width, width), 1) // group
    return (r == c).astype(BF16)


def _group_rms(x, gain_tiled, group):
    width = x.shape[-1]
    ss = _split_dot(x * x, _group_ones(width, group), 2)
    return x * lax.rsqrt(ss * (1.0 / group) + EPS) * gain_tiled


def _resident(shape, index=None):
    index = (0,) * len(shape) if index is None else index
    return pl.BlockSpec(shape, lambda *_: index, pipeline_mode=pl.Buffered(1))


def _layer(shape, layer, first=0):
    index = (layer, first) + (0,) * (len(shape) - 1)
    return pl.BlockSpec((None,) + tuple(shape), lambda *_: index,
                        pipeline_mode=pl.Buffered(1))


def _params(sem):
    return pltpu.CompilerParams(dimension_semantics=sem, vmem_limit_bytes=VMEM_LIMIT)


def _swiglu_rows(x, gain_ref, wg_ref, wu_ref, wd_ref, h_ref):
    n = _rms(x, gain_ref[...]).astype(BF16)
    for c in range(D_FF // FF_CHUNK):
        sl = slice(c * FF_CHUNK, (c + 1) * FF_CHUNK)
        g = jnp.dot(n, wg_ref[:, sl], preferred_element_type=F32)
        u = jnp.dot(n, wu_ref[:, sl], preferred_element_type=F32)
        h_ref[:, sl] = (g * jax.nn.sigmoid(g) * u).astype(BF16)
    y = jnp.dot(h_ref[...], wd_ref[...], preferred_element_type=F32)
    return x + 0.5 * y


def _ffn_kernel(x_ref, gain_ref, wg_ref, wu_ref, wd_ref, o_ref, h_ref):
    o_ref[...] = _swiglu_rows(x_ref[...], gain_ref, wg_ref, wu_ref, wd_ref, h_ref)


def _mix_ffn_kernel(x_ref, main_ref, memo_ref, wo_main_ref, wo_mem_ref,
                    gain_ref, wg_ref, wu_ref, wd_ref, o_ref, h_ref):
    x = (x_ref[...]
         + jnp.dot(main_ref[...], wo_main_ref[...], preferred_element_type=F32)
         + jnp.dot(memo_ref[...], wo_mem_ref[...], preferred_element_type=F32))
    o_ref[...] = _swiglu_rows(x, gain_ref, wg_ref, wu_ref, wd_ref, h_ref)


def _row_spec(width):
    return pl.BlockSpec((ROW_TILE, width), lambda i: (i, 0))


def _ffn_weight_specs(layer):
    return [_layer((1, D_MODEL), layer), _layer((D_MODEL, D_FF), layer),
            _layer((D_MODEL, D_FF), layer), _layer((D_FF, D_MODEL), layer)]


def _ffn(layer, x, gains, wg, wu, wd):
    return pl.pallas_call(
        _ffn_kernel,
        grid=(TOKENS // ROW_TILE,),
        in_specs=[_row_spec(D_MODEL)] + _ffn_weight_specs(layer),
        out_specs=_row_spec(D_MODEL),
        out_shape=jax.ShapeDtypeStruct((TOKENS, D_MODEL), F32),
        scratch_shapes=[pltpu.VMEM((ROW_TILE, D_FF), BF16)],
        compiler_params=_params(("parallel",)),
        name="ffn",
    )(x, gains, wg, wu, wd)


def _mix_ffn(layer, x, main, memo, w_out, gains, wg, wu, wd):
    assert MAIN_WIDTH % MEM_WIDTH == 0
    return pl.pallas_call(
        _mix_ffn_kernel,
        grid=(TOKENS // ROW_TILE,),
        in_specs=[_row_spec(D_MODEL), _row_spec(MAIN_WIDTH), _row_spec(MEM_WIDTH),
                  _layer((MAIN_WIDTH, D_MODEL), layer),
                  _layer((MEM_WIDTH, D_MODEL), layer, MAIN_WIDTH // MEM_WIDTH)]
        + _ffn_weight_specs(layer),
        out_specs=_row_spec(D_MODEL),
        out_shape=jax.ShapeDtypeStruct((TOKENS, D_MODEL), F32),
        scratch_shapes=[pltpu.VMEM((ROW_TILE, D_FF), BF16)],
        compiler_params=_params(("parallel",)),
        name="mix_ffn",
    )(x, main, memo, w_out, w_out, gains, wg, wu, wd)


def _inproj_a_kernel(layer, x_ref, gain_ref, w_ref, lbl_ref,
                     q_ref, lf_ref, i_ref, g_ref, qm_ref):
    n = _rms(x_ref[...], gain_ref[...]).astype(BF16)
    rows = [lbl_ref[r:r + 1, :] for r in range(N_A_LAYERS)]
    top = functools.reduce(jnp.maximum, rows)
    exps = [jnp.exp(r - top) for r in rows]
    total = functools.reduce(jnp.add, exps)
    lb = jnp.zeros_like(top)
    for r in range(1, layer + 1):
        lb = lb + exps[r] / total

    w = MAIN_WIDTH
    q_raw = jnp.dot(n, w_ref[:, 0:w], preferred_element_type=F32)
    q_ref[...] = q_raw * jax.nn.sigmoid(q_raw)
    f_raw = jnp.dot(n, w_ref[:, w:2 * w], preferred_element_type=F32)
    lf_ref[...] = jnp.log(lb + (1.0 - lb) * jax.nn.sigmoid(f_raw))
    i_ref[...] = jnp.dot(n, w_ref[:, 2 * w:3 * w], preferred_element_type=F32).astype(BF16)
    g_raw = jnp.dot(n, w_ref[:, 3 * w:4 * w], preferred_element_type=F32)
    g_ref[...] = g_raw * jax.nn.sigmoid(g_raw)
    qm_ref[...] = jnp.dot(n, w_ref[:, 4 * w:], preferred_element_type=F32)


def _inproj_a(layer, x, gains, w_in, lb_logits):
    a_in = 4 * MAIN_WIDTH + MEM_WIDTH
    return pl.pallas_call(
        functools.partial(_inproj_a_kernel, layer),
        grid=(TOKENS // ROW_TILE,),
        in_specs=[_row_spec(D_MODEL), _layer((1, D_MODEL), layer),
                  _layer((D_MODEL, a_in), layer), _resident((N_A_LAYERS, MAIN_WIDTH))],
        out_specs=[_row_spec(MAIN_WIDTH), _row_spec(MAIN_WIDTH), _row_spec(MAIN_WIDTH),
                   _row_spec(MAIN_WIDTH), _row_spec(MEM_WIDTH)],
        out_shape=[jax.ShapeDtypeStruct((TOKENS, MAIN_WIDTH), F32),
                   jax.ShapeDtypeStruct((TOKENS, MAIN_WIDTH), F32),
                   jax.ShapeDtypeStruct((TOKENS, MAIN_WIDTH), BF16),
                   jax.ShapeDtypeStruct((TOKENS, MAIN_WIDTH), F32),
                   jax.ShapeDtypeStruct((TOKENS, MEM_WIDTH), F32)],
        compiler_params=_params(("parallel",)),
        name="inproj_a",
    )(x, gains, w_in, lb_logits)


def _inproj_b_kernel(x_ref, gain_ref, w_ref, qg_ref, qbias_ref, qa_ref, gate_ref, qm_ref):
    n = _rms(x_ref[...], gain_ref[...]).astype(BF16)
    w = MAIN_WIDTH
    scale = FOX_HEAD_DIM ** -0.5 * LOG2E
    pair_w = 2 * FOX_HEAD_DIM
    q_raw = [jnp.dot(n, w_ref[:, c * 256:(c + 1) * 256], preferred_element_type=F32)
             for c in range(w // 256)]
    gate = jnp.dot(n, w_ref[:, w:2 * w], preferred_element_type=F32)
    qm_ref[...] = jnp.dot(n, w_ref[:, 2 * w:], preferred_element_type=F32)
    for c in range(w // 256):
        sl = slice(c * 256, (c + 1) * 256)
        q_n = (_group_rms(q_raw[c], qg_ref[:, sl], FOX_HEAD_DIM) * scale).astype(BF16)
        for pp in range(256 // pair_w):
            lo = c * 256 + pp * pair_w
            for hh in range(2):
                dst = 2 * lo + hh * pair_w
                qa_ref[:, dst:dst + pair_w] = _with_bias_lanes(
                    q_n[:, pp * pair_w:(pp + 1) * pair_w], qbias_ref[:, lo:lo + pair_w], hh)
    gate_ref[...] = jax.nn.sigmoid(gate)


def _inproj_b(layer, x, gains, w_in, q_gain_tiled, q_bias):
    b_in = 2 * MAIN_WIDTH + MEM_WIDTH
    j = layer - N_A_LAYERS
    return pl.pallas_call(
        _inproj_b_kernel,
        grid=(TOKENS // ROW_TILE,),
        in_specs=[_row_spec(D_MODEL), _layer((1, D_MODEL), layer),
                  _layer((D_MODEL, b_in), j), _layer((1, MAIN_WIDTH), j),
                  _row_spec(MAIN_WIDTH)],
        out_specs=[_row_spec(2 * MAIN_WIDTH), _row_spec(MAIN_WIDTH), _row_spec(MEM_WIDTH)],
        out_shape=[jax.ShapeDtypeStruct((TOKENS, 2 * MAIN_WIDTH), BF16),
                   jax.ShapeDtypeStruct((TOKENS, MAIN_WIDTH), F32),
                   jax.ShapeDtypeStruct((TOKENS, MEM_WIDTH), F32)],
        compiler_params=_params(("parallel",)),
        name="inproj_b",
    )(x, gains, w_in, q_gain_tiled, q_bias)


def _hgrn_kernel(q_ref, lf_ref, v_ref, g_ref, gain_ref, o_ref,
                 cpad_ref, kpad_ref, vpad_ref):
    C, SB, HD = HG_CHUNK, HG_SUB, HG_HEAD_DIM
    n_sub = C // SB
    r_i = lax.broadcasted_iota(jnp.int32, (C, C), 0)
    c_i = lax.broadcasted_iota(jnp.int32, (C, C), 1)
    tril = (r_i >= c_i).astype(BF16)
    gain = gain_ref[...]

    def intra_exact(q, k, v16, c):
        outs = [jnp.zeros((SB, HD), F32)]
        for b in range(1, n_sub):
            lo = b * SB
            ref = c[lo - 1:lo, :]
            qb = (q[lo:lo + SB, :] * jnp.exp(c[lo:lo + SB, :] - ref)).astype(BF16)
            kb = (k[0:lo, :] * jnp.exp(ref - c[0:lo, :])).astype(BF16)
            p_b = lax.dot_general(qb, kb, (((1,), (1,)), ((), ())),
                                  preferred_element_type=F32)
            outs.append(jnp.dot(p_b.astype(BF16), v16[0:lo, :], preferred_element_type=F32))
        o = jnp.concatenate(outs, axis=0)
        ones = jnp.ones((HD, HD), BF16)
        row_in_block = lax.broadcasted_iota(jnp.int32, (C, HD), 0) % SB
        cpad_ref[SB:SB + C, :] = c
        kpad_ref[SB:SB + C, :] = k
        vpad_ref[SB:SB + C, :] = v16.astype(F32)
        for d in range(SB):
            cs = cpad_ref[SB - d:SB - d + C, :]
            ks = kpad_ref[SB - d:SB - d + C, :]
            vs = vpad_ref[SB - d:SB - d + C, :]
            wgt = jnp.where(row_in_block >= d, q * ks * jnp.exp(c - cs), 0.0)
            s = jnp.dot(wgt.astype(BF16), ones, preferred_element_type=F32)
            o = o + s * vs
        return o

    def chunk(n, st, intra):
        r0 = pl.multiple_of(n * C, C)
        lf = lf_ref[0, pl.ds(r0, C), :]
        q = q_ref[0, pl.ds(r0, C), :]
        v16 = v_ref[0, pl.ds(r0, C), :]
        k = 1.0 - jnp.exp(lf)
        c = None
        rem = lf
        for p in range(3):
            piece = rem.astype(BF16)
            term = jnp.dot(tril, piece, preferred_element_type=F32)
            c = term if c is None else c + term
            rem = rem - piece.astype(F32)
        c_end = c[C - 1:C, :]

        qe = (q * jnp.exp(c)).astype(BF16)
        o = lax.dot_general(qe, st.astype(BF16), (((1,), (1,)), ((), ())),
                            preferred_element_type=F32)
        kd = (k * jnp.exp(c_end - c)).astype(BF16)
        upd = lax.dot_general(v16, kd, (((0,), (0,)), ((), ())),
                              preferred_element_type=F32)
        o = o + intra(q, k, v16, c)
        o_ref[0, pl.ds(r0, C), :] = (_rms(o, gain) * g_ref[0, pl.ds(r0, C), :]).astype(BF16)
        return st * jnp.exp(c_end) + upd

    G = HG_GROUP
    R = G * C
    gr = lax.broadcasted_iota(jnp.int32, (R, R), 0)
    gc = lax.broadcasted_iota(jnp.int32, (R, R), 1)
    tril_group = ((gr // C == gc // C) & (gr >= gc)).astype(BF16)
    sub_of_row = lax.broadcasted_iota(jnp.int32, (C, HD), 0) // SB
    causal = r_i >= c_i

    def stage_cumsum(i):
        lf = lf_ref[0, i * R:(i + 1) * R, :]
        c = None
        rem = lf
        for p in range(3):
            piece = rem.astype(BF16)
            term = jnp.dot(tril_group, piece, preferred_element_type=F32)
            c = term if c is None else c + term
            rem = rem - piece.astype(F32)
        return lf, c

    def stage_scores(i, lf, c):
        q = q_ref[0, i * R:(i + 1) * R, :]
        v16 = v_ref[0, i * R:(i + 1) * R, :]
        k = 1.0 - jnp.exp(lf)
        lhs, rhs, c_ends = [], [], []
        for g in range(G):
            cg = c[g * C:(g + 1) * C, :]
            qg = q[g * C:(g + 1) * C, :]
            kg = k[g * C:(g + 1) * C, :]
            refs = [jnp.zeros((1, HD), F32)] + [cg[b * SB - 1:b * SB, :] for b in range(1, n_sub)]
            own_ref = jnp.concatenate([jnp.broadcast_to(r, (SB, HD)) for r in refs], axis=0)
            k_own = kg * jnp.exp(own_ref - cg)
            lhs.append(jnp.concatenate(
                [(qg * jnp.exp(jnp.minimum(cg - r, 0.0))).astype(BF16) for r in refs], axis=1))
            rhs.append(jnp.concatenate(
                [jnp.where(sub_of_row == b, k_own, 0.0).astype(BF16) for b in range(n_sub)],
                axis=1))
            c_ends.append(cg[C - 1:C, :])
        c_end_rows = jnp.concatenate([jnp.broadcast_to(e, (C, HD)) for e in c_ends], axis=0)
        qe = (q * jnp.exp(c)).astype(BF16)
        kd = (k * jnp.exp(c_end_rows - c)).astype(BF16)

        scores = [lax.dot_general(lhs[g], rhs[g], (((1,), (1,)), ((), ())),
                                  preferred_element_type=F32) for g in range(G)]
        upds = [lax.dot_general(v16[g * C:(g + 1) * C, :], kd[g * C:(g + 1) * C, :],
                                (((0,), (0,)), ((), ())), preferred_element_type=F32)
                for g in range(G)]
        return scores, upds, qe, c_ends

    def stage_output(i, prepared, st):
        scores, upds, qe, c_ends = prepared
        v16 = v_ref[0, i * R:(i + 1) * R, :]
        intra = [jnp.dot(jnp.where(causal, scores[g], 0.0).astype(BF16),
                         v16[g * C:(g + 1) * C, :], preferred_element_type=F32)
                 for g in range(G)]
        outs = []
        for g in range(G):
            inter = lax.dot_general(qe[g * C:(g + 1) * C, :], st.astype(BF16),
                                    (((1,), (1,)), ((), ())), preferred_element_type=F32)
            outs.append(inter + intra[g])
            st = st * jnp.exp(c_ends[g]) + upds[g]
        o = jnp.concatenate(outs, axis=0)
        rows = slice(i * R, (i + 1) * R)
        o_ref[0, rows, :] = (_rms(o, gain) * g_ref[0, rows, :]).astype(BF16)
        return st

    def run_factored():
        n_groups = SEQ // R
        st = jnp.zeros((HD, HD), F32)
        prepared = stage_scores(0, *stage_cumsum(0))
        for i in range(n_groups):
            if i + 1 < n_groups:
                ahead = stage_cumsum(i + 1)
            st = stage_output(i, prepared, st)
            if i + 1 < n_groups:
                prepared = stage_scores(i + 1, *ahead)

    def run_exact():
        def body(n, st):
            return chunk(n, st, intra_exact)
        lax.fori_loop(0, SEQ // C, body, jnp.zeros((HD, HD), F32))

    blk = lax.broadcasted_iota(jnp.int32, (SEQ // SB, SEQ), 0)
    tok = lax.broadcasted_iota(jnp.int32, (SEQ // SB, SEQ), 1) // SB
    block_decay = jnp.dot((blk == tok).astype(BF16), lf_ref[0].astype(BF16),
                          preferred_element_type=F32)
    mild = jnp.min(block_decay) >= -HG_SAFE_DECAY

    @pl.when(mild)
    def _():
        run_factored()

    @pl.when(jnp.logical_not(mild))
    def _():
        zpad = jnp.zeros((SB, HD), F32)
        cpad_ref[0:SB, :] = zpad
        kpad_ref[0:SB, :] = zpad
        vpad_ref[0:SB, :] = zpad
        run_exact()


def _hgrn(layer, q, lf, v, g, o_gains):
    spec = pl.BlockSpec((1, SEQ, HG_HEAD_DIM), lambda b, h: (b, 0, h))
    return pl.pallas_call(
        _hgrn_kernel,
        grid=(BATCH, HG_HEADS),
        in_specs=[spec, spec, spec, spec, _layer((1, HG_HEAD_DIM), layer)],
        out_specs=spec,
        out_shape=jax.ShapeDtypeStruct((BATCH, SEQ, MAIN_WIDTH), BF16),
        scratch_shapes=[pltpu.VMEM((HG_SUB + HG_CHUNK, HG_HEAD_DIM), F32),
                        pltpu.VMEM((HG_SUB + HG_CHUNK, HG_HEAD_DIM), F32),
                        pltpu.VMEM((HG_SUB + HG_CHUNK, HG_HEAD_DIM), F32)],
        compiler_params=_params(("parallel", "parallel")),
        name="hgrn",
    )(q, lf, v, g, o_gains)


FOX_BIAS_PIECES = 3
FOX_PIECE_STRIDE = 16
LOG2E = 1.4426950408889634


def _bias_slot_layout():
    lane = lax.broadcasted_iota(jnp.int32, (V7X_LANES, MAIN_WIDTH), 1)
    pair = lane // (2 * FOX_HEAD_DIM)
    half = (lane // FOX_HEAD_DIM) % 2
    head_here = 2 * pair + (1 - half)
    slot = lane % FOX_HEAD_DIM
    return head_here, slot


def _place_bias(packed):
    head_here, slot = _bias_slot_layout()
    src = lax.broadcasted_iota(jnp.int32, (V7X_LANES, MAIN_WIDTH), 0)
    src_piece, src_head = src // FOX_PIECE_STRIDE, src % FOX_PIECE_STRIDE
    hit = (src_piece < FOX_BIAS_PIECES) & (src_head == head_here)
    place = jnp.concatenate([(hit & (slot == src_piece)).astype(BF16),
                             (hit & (slot == src_piece + FOX_BIAS_PIECES)).astype(BF16)],
                            axis=1)
    return jnp.dot(packed, place, preferred_element_type=F32)


def _with_bias_lanes(pair_vals, pair_bias, head_in_pair):
    own = (lax.broadcasted_iota(jnp.int32, pair_vals.shape, 1) // FOX_HEAD_DIM) == head_in_pair
    return jnp.where(own, pair_vals, pair_bias)


def _kv_kernel(x_ref, gain_ref, wk_ref, wv_ref, wf_ref, fb_ref, kg_ref,
               ka_ref, vt_ref, qbias_ref, carry_ref, v_ref):
    @pl.when(pl.program_id(1) == 0)
    def _():
        carry_ref[...] = jnp.zeros_like(carry_ref)

    n = _rms(x_ref[0], gain_ref[...]).astype(BF16)

    z = jnp.dot(n, wf_ref[...], preferred_element_type=F32) + fb_ref[...]
    v_ref[...] = jnp.dot(n, wv_ref[...], preferred_element_type=F32)
    k_raw = [jnp.dot(n, wk_ref[:, c * 256:(c + 1) * 256], preferred_element_type=F32)
             for c in range(MAIN_WIDTH // 256)]
    log_f = jnp.minimum(z, 0.0) - jnp.log1p(jnp.exp(-jnp.abs(z)))
    seg = KV_CUM_SEG
    tril = (lax.broadcasted_iota(jnp.int32, (seg, seg), 0)
            >= lax.broadcasted_iota(jnp.int32, (seg, seg), 1)).astype(BF16)
    local = []
    for sgi in range(KV_TILE // seg):
        rem = log_f[sgi * seg:(sgi + 1) * seg, :]
        acc = None
        for p in range(3):
            piece = rem.astype(BF16)
            term = jnp.dot(tril, piece, preferred_element_type=F32)
            acc = term if acc is None else acc + term
            rem = rem - piece.astype(F32)
        local.append(acc)
    last = carry_ref[...]
    parts = []
    for acc in local:
        parts.append(acc + last)
        last = parts[-1][seg - 1:seg, :]
    carry_ref[...] = last
    cum = jnp.concatenate(parts, axis=0)

    lane = lax.broadcasted_iota(jnp.int32, cum.shape, 1)
    rem = jnp.where(lane < FOX_HEADS, cum * LOG2E, 0.0)
    packed = None
    for p in range(FOX_BIAS_PIECES):
        piece = rem.astype(BF16).astype(F32)
        moved = pltpu.roll(piece, p * FOX_PIECE_STRIDE, axis=1) if p else piece
        packed = moved if packed is None else packed + moved
        rem = rem - piece
    placed = _place_bias(packed.astype(BF16))
    _, slot = _bias_slot_layout()
    slot_row = slot[0:1, :]
    n_p = FOX_BIAS_PIECES
    k_bias = (placed[:, :MAIN_WIDTH]
              + ((slot_row >= n_p) & (slot_row < 2 * n_p)).astype(F32)).astype(BF16)
    qbias_ref[0] = (placed[:, MAIN_WIDTH:] - (slot_row < n_p).astype(F32)).astype(BF16)

    pair_w = 2 * FOX_HEAD_DIM
    for c in range(MAIN_WIDTH // 256):
        sl = slice(c * 256, (c + 1) * 256)
        k_n = _group_rms(k_raw[c], kg_ref[:, sl], FOX_HEAD_DIM).astype(BF16)
        for pp in range(256 // pair_w):
            lo = c * 256 + pp * pair_w
            for hh in range(2):
                dst = 2 * lo + hh * pair_w
                ka_ref[0, :, dst:dst + pair_w] = _with_bias_lanes(
                    k_n[:, pp * pair_w:(pp + 1) * pair_w], k_bias[:, lo:lo + pair_w], hh)
    vt_ref[0] = v_ref[...].T.astype(BF16)


def _kv(x, gain, w_kv, wf, f_bias, k_gain_tiled):
    row = lambda w: pl.BlockSpec((1, KV_TILE, w), lambda b, i: (b, i, 0))
    return pl.pallas_call(
        _kv_kernel,
        grid=(BATCH, SEQ // KV_TILE),
        in_specs=[row(D_MODEL), _resident((1, D_MODEL)),
                  _resident((D_MODEL, MAIN_WIDTH), (0, 0)),
                  _resident((D_MODEL, MAIN_WIDTH), (0, 1)), _resident((D_MODEL, V7X_LANES)),
                  _resident((1, V7X_LANES)), _resident((1, MAIN_WIDTH))],
        out_specs=[row(2 * MAIN_WIDTH),
                   pl.BlockSpec((1, MAIN_WIDTH, KV_TILE), lambda b, i: (b, 0, i)),
                   row(MAIN_WIDTH)],
        out_shape=[jax.ShapeDtypeStruct((BATCH, SEQ, 2 * MAIN_WIDTH), BF16),
                   jax.ShapeDtypeStruct((BATCH, MAIN_WIDTH, SEQ), BF16),
                   jax.ShapeDtypeStruct((BATCH, SEQ, MAIN_WIDTH), BF16)],
        scratch_shapes=[pltpu.VMEM((1, V7X_LANES), F32),
                        pltpu.VMEM((KV_TILE, MAIN_WIDTH), F32)],
        compiler_params=_params(("parallel", "arbitrary")),
        name="kv",
    )(x, gain, w_kv, w_kv, wf, f_bias, k_gain_tiled)


def _fox_kernel(qa_ref, ka_ref, vt_ref, gate_ref, o_ref, s_ref, m_ref, acc_ref):
    T, HD = FOX_TILE, FOX_HEAD_DIM
    W = 2 * HD
    ones_rows = jnp.ones((ONES_ROWS, T), BF16)
    visible = (lax.broadcasted_iota(jnp.int32, (T, T), 0)
               <= lax.broadcasted_iota(jnp.int32, (T, T), 1))

    def issue_scores(slot, i, j):
        for hh in range(2):
            s_ref[slot, hh] = lax.dot_general(
                ka_ref[0, j * T:(j + 1) * T, hh * W:(hh + 1) * W],
                qa_ref[0, i * T:(i + 1) * T, hh * W:(hh + 1) * W],
                (((1,), (1,)), ((), ())), preferred_element_type=F32)

    def consume(slot, i, j):
        probs, alphas = [], []
        for hh in range(2):
            s = s_ref[slot, hh]
            if j == i:
                s = jnp.where(visible, s, -jnp.inf)
            m_new = jnp.max(s, axis=0, keepdims=True)
            if j:
                m_old = m_ref[hh]
                m_new = jnp.maximum(m_old, m_new)
                alphas.append(jnp.exp2(m_old - m_new))
            probs.append(jnp.exp2(s - m_new).astype(BF16))
            m_ref[hh] = m_new
        for hh in range(2):
            vt = jnp.concatenate([vt_ref[0, hh * HD:(hh + 1) * HD, j * T:(j + 1) * T],
                                 ones_rows], axis=0)
            pv = jnp.dot(vt, probs[hh], preferred_element_type=F32)
            acc_ref[hh] = alphas[hh] * acc_ref[hh] + pv if j else pv
        if j == i:
            out_t = jnp.concatenate([acc_ref[hh, 0:HD, :] / acc_ref[hh, HD:HD + 1, :]
                                     for hh in range(2)], axis=0)
            rows = slice(i * T, (i + 1) * T)
            o_ref[0, rows, :] = (out_t.T * gate_ref[0, rows, :]).astype(BF16)

    blocks = [(i, j) for i in range(SEQ // T) for j in range(i + 1)]
    issue_scores(0, *blocks[0])
    for n, (i, j) in enumerate(blocks):
        if n + 1 < len(blocks):
            issue_scores((n + 1) % 2, *blocks[n + 1])
        consume(n % 2, i, j)


def _fox(qa, ka, vt, gate):
    T = FOX_TILE
    W = 2 * FOX_HEAD_DIM
    seq_spec = lambda w: pl.BlockSpec((1, SEQ, w), lambda b, p: (b, 0, p))
    return pl.pallas_call(
        _fox_kernel,
        grid=(BATCH, FOX_HEADS // 2),
        in_specs=[seq_spec(2 * W), seq_spec(2 * W),
                  pl.BlockSpec((1, W, SEQ), lambda b, p: (b, p, 0)),
                  seq_spec(W)],
        out_specs=seq_spec(W),
        out_shape=jax.ShapeDtypeStruct((BATCH, SEQ, MAIN_WIDTH), BF16),
        scratch_shapes=[pltpu.VMEM((2, 2, T, T), F32),
                        pltpu.VMEM((2, 1, T), F32),
                        pltpu.VMEM((2, FOX_HEAD_DIM + ONES_ROWS, T), F32)],
        compiler_params=_params(("parallel", "parallel")),
        name="fox",
    )(qa, ka, vt, gate)


def _memattn_kernel(qm_ref, mem_ref, mgain_ref, wkv_ref, qg_ref, kg_ref, o_ref,
                    kv_ref, km_ref, vmt_ref):
    W, HD = MEM_WIDTH, MEM_HEAD_DIM

    @pl.when(pl.program_id(1) == 0)
    def _():
        mem_n = _rms(mem_ref[0], mgain_ref[...]).astype(BF16)
        kv_ref[...] = jnp.dot(mem_n, wkv_ref[...], preferred_element_type=F32)
        km = _group_rms(kv_ref[:, :W], kg_ref[...], HD).astype(BF16)
        head_of_lane = lax.broadcasted_iota(jnp.int32, km.shape, 1) // HD
        vm_t = kv_ref[:, W:].T.astype(BF16)
        ones_rows = jnp.ones((ONES_ROWS, N_MEM), BF16)
        for h in range(MEM_HEADS):
            km_ref[h] = jnp.where(head_of_lane == h, km, jnp.zeros_like(km))
            vmt_ref[h] = jnp.concatenate([vm_t[h * HD:(h + 1) * HD, :], ones_rows], axis=0)

    scale = HD ** -0.5 * LOG2E
    qn = (_group_rms(qm_ref[0], qg_ref[...], HD) * scale).astype(BF16)
    scores = [lax.dot_general(km_ref[h], qn, (((1,), (1,)), ((), ())),
                              preferred_element_type=F32) for h in range(MEM_HEADS)]
    outs = []
    for h in range(MEM_HEADS):
        s = scores[h]
        p = jnp.exp2(s - jnp.max(s, axis=0, keepdims=True)).astype(BF16)
        r = jnp.dot(vmt_ref[h], p, preferred_element_type=F32)
        outs.append(r[0:HD, :] / r[HD:HD + 1, :])
    o_ref[0] = jnp.concatenate(outs, axis=0).T.astype(BF16)


def _memattn(layer, qm, mem, mem_gains, w_mem_kv, q_gains_tiled, k_gains_tiled):
    T = MEM_Q_TILE
    qspec = pl.BlockSpec((1, T, MEM_WIDTH), lambda b, i: (b, i, 0))
    return pl.pallas_call(
        _memattn_kernel,
        grid=(BATCH, SEQ // T),
        in_specs=[qspec,
                  pl.BlockSpec((1, N_MEM, D_MODEL), lambda b, i: (b, 0, 0)),
                  _layer((1, D_MODEL), layer), _layer((D_MODEL, 2 * MEM_WIDTH), layer),
                  _layer((1, MEM_WIDTH), layer), _layer((1, MEM_WIDTH), layer)],
        out_specs=qspec,
        out_shape=jax.ShapeDtypeStruct((BATCH, SEQ, MEM_WIDTH), BF16),
        scratch_shapes=[pltpu.VMEM((N_MEM, 2 * MEM_WIDTH), F32),
                        pltpu.VMEM((MEM_HEADS, N_MEM, MEM_WIDTH), BF16),
                        pltpu.VMEM((MEM_HEADS, MEM_HEAD_DIM + ONES_ROWS, N_MEM), BF16)],
        compiler_params=_params(("parallel", "arbitrary")),
        name="memattn",
    )(qm, mem, mem_gains, w_mem_kv, q_gains_tiled, k_gains_tiled)


def kernel(x, mem, ffn1_norm, ffn1_w_gate, ffn1_w_up, ffn1_w_down, mix_norm, mem_norm,
           w_mem_kv, mem_q_gain, mem_k_gain, w_in_a, hgrn_lb_logits, hgrn_o_gain,
           w_in_b, fox_q_gain, kv_norm, w_kv, fox_f_bias, fox_k_gain, w_out,
           ffn2_norm, ffn2_w_gate, ffn2_w_up, ffn2_w_down):
    bf = lambda t: t.astype(BF16)
    row = lambda t: t.reshape(1, -1).astype(F32)
    rows = lambda t, reps=1: jnp.tile(t.astype(F32), (1, reps)).reshape(t.shape[0], 1, -1)
    x = x.reshape(TOKENS, D_MODEL).astype(F32)
    mem = mem.astype(F32)
    lb_logits = hgrn_lb_logits.astype(F32)
    ffn1 = (rows(ffn1_norm), bf(ffn1_w_gate), bf(ffn1_w_up), bf(ffn1_w_down))
    ffn2 = (rows(ffn2_norm), bf(ffn2_w_gate), bf(ffn2_w_up), bf(ffn2_w_down))
    mix_gains, w_a, w_b, w_o = rows(mix_norm), bf(w_in_a), bf(w_in_b), bf(w_out)
    o_gains, fox_q_gains = rows(hgrn_o_gain), rows(fox_q_gain, FOX_HEADS)
    mem_args = (rows(mem_norm), bf(w_mem_kv), rows(mem_q_gain, MEM_HEADS),
                rows(mem_k_gain, MEM_HEADS))

    for l in range(DEPTH):
        x = _ffn(l, x, *ffn1)
        if l < N_A_LAYERS:
            qs, lf, iv, sg, qm = _inproj_a(l, x, mix_gains, w_a, lb_logits)
            to3 = lambda t: t.reshape(BATCH, SEQ, MAIN_WIDTH)
            main = _hgrn(l, to3(qs), to3(lf), to3(iv), to3(sg), o_gains)
        else:
            qa, gate, qm = _inproj_b(l, x, mix_gains, w_b, fox_q_gains,
                                     q_bias.reshape(TOKENS, MAIN_WIDTH))
            main = _fox(qa.reshape(BATCH, SEQ, 2 * MAIN_WIDTH), k_aug, v_t,
                        gate.reshape(BATCH, SEQ, MAIN_WIDTH))
        memo = _memattn(l, qm.reshape(BATCH, SEQ, MEM_WIDTH), mem, *mem_args)
        x = _mix_ffn(l, x, main.reshape(TOKENS, MAIN_WIDTH), memo.reshape(TOKENS, MEM_WIDTH),
                     w_o, *ffn2)
        if l == N_A_LAYERS - 1:
            pad = V7X_LANES - FOX_HEADS
            wf = jnp.pad(w_kv[:, 2 * MAIN_WIDTH:], ((0, 0), (0, pad)))
            fb = jnp.pad(fox_f_bias.astype(F32), (0, pad)).reshape(1, -1)
            k_aug, v_t, q_bias = _kv(x.reshape(BATCH, SEQ, D_MODEL), row(kv_norm),
                                     bf(w_kv), bf(wf), fb,
                                     row(jnp.tile(fox_k_gain, FOX_HEADS)))
    return x.reshape(BATCH, SEQ, D_MODEL)
```

```python
import functools

import jax
import jax.numpy as jnp
from jax import lax
from jax.experimental import pallas as pl
from jax.experimental.pallas import tpu as pltpu

F32 = jnp.float32
BF16 = jnp.bfloat16

D_MODEL = 1024
BATCH = 8
SEQ = 2048
DEPTH = 4
N_MEM = 256
N_A_LAYERS = DEPTH // 2
MAIN_WIDTH = 768
MEM_WIDTH = 256
HG_HEAD_DIM = 128
HG_HEADS = MAIN_WIDTH // HG_HEAD_DIM
FOX_HEAD_DIM = 64
FOX_HEADS = MAIN_WIDTH // FOX_HEAD_DIM
MEM_HEADS = 4
MEM_HEAD_DIM = MEM_WIDTH // MEM_HEADS
D_FF = 2816
EPS = 1e-6
TOKENS = BATCH * SEQ

V7X_LANES = 128
V7X_VMEM_BYTES = 64 * 1024 * 1024

ROW_TILE = 512
FF_CHUNK = 256
HG_CHUNK = 64
HG_SUB = 16
HG_SAFE_DECAY = 60.0
HG_GROUP = 4
FOX_TILE = 512
ONES_ROWS = 16
MEM_Q_TILE = 2048
KV_TILE = 512
KV_CUM_SEG = 256
VMEM_LIMIT = 56 * 1024 * 1024


def _rms(x, gain):
    ms = jnp.mean(x * x, axis=-1, keepdims=True)
    return x * lax.rsqrt(ms + EPS) * gain


def _mm(a, w):
    return jnp.dot(a, w.astype(BF16), preferred_element_type=F32)


def _split_dot(x, rhs01, parts):
    acc = None
    rem = x
    for p in range(parts):
        piece = rem.astype(BF16)
        term = jnp.dot(piece, rhs01, preferred_element_type=F32)
        acc = term if acc is None else acc + term
        if p + 1 < parts:
            rem = rem - piece.astype(F32)
    return acc


def _group_ones(width, group):
    r = lax.broadcasted_iota(jnp.int32, (width, width), 0) // group
    c = lax.broadcasted_iota(jnp.int32, (width, width), 1) // group
    return (r == c).astype(BF16)


def _group_rms(x, gain_tiled, group):
    width = x.shape[-1]
    ss = _split_dot(x * x, _group_ones(width, group), 2)
    return x * lax.rsqrt(ss * (1.0 / group) + EPS) * gain_tiled


def _resident(shape, index=None):
    index = (0,) * len(shape) if index is None else index
    return pl.BlockSpec(shape, lambda *_: index, pipeline_mode=pl.Buffered(1))


def _layer(shape, layer, first=0):
    index = (layer, first) + (0,) * (len(shape) - 1)
    return pl.BlockSpec((None,) + tuple(shape), lambda *_: index,
                        pipeline_mode=pl.Buffered(1))


def _params(sem):
    return pltpu.CompilerParams(dimension_semantics=sem, vmem_limit_bytes=VMEM_LIMIT)


def _swiglu_rows(x, gain_ref, wg_ref, wu_ref, wd_ref, h_ref):
    n = _rms(x, gain_ref[...]).astype(BF16)
    for c in range(D_FF // FF_CHUNK):
        sl = slice(c * FF_CHUNK, (c + 1) * FF_CHUNK)
        g = _mm(n, wg_ref[:, sl])
        u = _mm(n, wu_ref[:, sl])
        h_ref[:, sl] = (g * jax.nn.sigmoid(g) * u).astype(BF16)
    y = _mm(h_ref[...], wd_ref[...])
    return x + 0.5 * y


def _ffn_kernel(x_ref, gain_ref, wg_ref, wu_ref, wd_ref, o_ref, h_ref):
    o_ref[...] = _swiglu_rows(x_ref[...], gain_ref, wg_ref, wu_ref, wd_ref, h_ref)


def _mix_ffn_kernel(x_ref, main_ref, memo_ref, wo_main_ref, wo_mem_ref,
                    gain_ref, wg_ref, wu_ref, wd_ref, o_ref, h_ref):
    x = (x_ref[...]
         + _mm(main_ref[...], wo_main_ref[...])
         + _mm(memo_ref[...], wo_mem_ref[...]))
    o_ref[...] = _swiglu_rows(x, gain_ref, wg_ref, wu_ref, wd_ref, h_ref)


def _row_spec(width):
    return pl.BlockSpec((ROW_TILE, width), lambda i: (i, 0))


def _ffn_weight_specs(layer):
    return [_layer((1, D_MODEL), layer), _layer((D_MODEL, D_FF), layer),
            _layer((D_MODEL, D_FF), layer), _layer((D_FF, D_MODEL), layer)]


def _ffn(layer, x, gains, wg, wu, wd):
    return pl.pallas_call(
        _ffn_kernel,
        grid=(TOKENS // ROW_TILE,),
        in_specs=[_row_spec(D_MODEL)] + _ffn_weight_specs(layer),
        out_specs=_row_spec(D_MODEL),
        out_shape=jax.ShapeDtypeStruct((TOKENS, D_MODEL), F32),
        scratch_shapes=[pltpu.VMEM((ROW_TILE, D_FF), BF16)],
        compiler_params=_params(("parallel",)),
        name="ffn",
    )(x, gains, wg, wu, wd)


def _mix_ffn(layer, x, main, memo, w_out, gains, wg, wu, wd):
    assert MAIN_WIDTH % MEM_WIDTH == 0
    return pl.pallas_call(
        _mix_ffn_kernel,
        grid=(TOKENS // ROW_TILE,),
        in_specs=[_row_spec(D_MODEL), _row_spec(MAIN_WIDTH), _row_spec(MEM_WIDTH),
                  _layer((MAIN_WIDTH, D_MODEL), layer),
                  _layer((MEM_WIDTH, D_MODEL), layer, MAIN_WIDTH // MEM_WIDTH)]
        + _ffn_weight_specs(layer),
        out_specs=_row_spec(D_MODEL),
        out_shape=jax.ShapeDtypeStruct((TOKENS, D_MODEL), F32),
        scratch_shapes=[pltpu.VMEM((ROW_TILE, D_FF), BF16)],
        compiler_params=_params(("parallel",)),
        name="mix_ffn",
    )(x, main, memo, w_out, w_out, gains, wg, wu, wd)


def _inproj_a_kernel(layer, x_ref, gain_ref, w_ref, lbl_ref,
                     q_ref, lf_ref, i_ref, g_ref, qm_ref):
    n = _rms(x_ref[...], gain_ref[...]).astype(BF16)
    rows = [lbl_ref[r:r + 1, :] for r in range(N_A_LAYERS)]
    top = functools.reduce(jnp.maximum, rows)
    exps = [jnp.exp(r - top) for r in rows]
    total = functools.reduce(jnp.add, exps)
    lb = jnp.zeros_like(top)
    for r in range(1, layer + 1):
        lb = lb + exps[r] / total

    w = MAIN_WIDTH
    q_raw = _mm(n, w_ref[:, 0:w])
    q_ref[...] = q_raw * jax.nn.sigmoid(q_raw)
    f_raw = _mm(n, w_ref[:, w:2 * w])
    lf_ref[...] = jnp.log(lb + (1.0 - lb) * jax.nn.sigmoid(f_raw))
    i_ref[...] = _mm(n, w_ref[:, 2 * w:3 * w]).astype(BF16)
    g_raw = _mm(n, w_ref[:, 3 * w:4 * w])
    g_ref[...] = g_raw * jax.nn.sigmoid(g_raw)
    qm_ref[...] = _mm(n, w_ref[:, 4 * w:])


def _inproj_a(layer, x, gains, w_in, lb_logits):
    a_in = 4 * MAIN_WIDTH + MEM_WIDTH
    return pl.pallas_call(
        functools.partial(_inproj_a_kernel, layer),
        grid=(TOKENS // ROW_TILE,),
        in_specs=[_row_spec(D_MODEL), _layer((1, D_MODEL), layer),
                  _layer((D_MODEL, a_in), layer), _resident((N_A_LAYERS, MAIN_WIDTH))],
        out_specs=[_row_spec(MAIN_WIDTH), _row_spec(MAIN_WIDTH), _row_spec(MAIN_WIDTH),
                   _row_spec(MAIN_WIDTH), _row_spec(MEM_WIDTH)],
        out_shape=[jax.ShapeDtypeStruct((TOKENS, MAIN_WIDTH), F32),
                   jax.ShapeDtypeStruct((TOKENS, MAIN_WIDTH), F32),
                   jax.ShapeDtypeStruct((TOKENS, MAIN_WIDTH), BF16),
                   jax.ShapeDtypeStruct((TOKENS, MAIN_WIDTH), F32),
                   jax.ShapeDtypeStruct((TOKENS, MEM_WIDTH), F32)],
        compiler_params=_params(("parallel",)),
        name="inproj_a",
    )(x, gains, w_in, lb_logits)


def _inproj_b_kernel(x_ref, gain_ref, w_ref, qg_ref, qbias_ref, qa_ref, gate_ref, qm_ref):
    n = _rms(x_ref[...], gain_ref[...]).astype(BF16)
    w = MAIN_WIDTH
    scale = FOX_HEAD_DIM ** -0.5 * LOG2E
    pair_w = 2 * FOX_HEAD_DIM
    q_raw = [_mm(n, w_ref[:, c * 256:(c + 1) * 256]) for c in range(w // 256)]
    gate = _mm(n, w_ref[:, w:2 * w])
    qm_ref[...] = _mm(n, w_ref[:, 2 * w:])
    for c in range(w // 256):
        sl = slice(c * 256, (c + 1) * 256)
        q_n = (_group_rms(q_raw[c], qg_ref[:, sl], FOX_HEAD_DIM) * scale).astype(BF16)
        for pp in range(256 // pair_w):
            lo = c * 256 + pp * pair_w
            for hh in range(2):
                dst = 2 * lo + hh * pair_w
                qa_ref[:, dst:dst + pair_w] = _with_bias_lanes(
                    q_n[:, pp * pair_w:(pp + 1) * pair_w], qbias_ref[:, lo:lo + pair_w], hh)
    gate_ref[...] = jax.nn.sigmoid(gate)


def _inproj_b(layer, x, gains, w_in, q_gain_tiled, q_bias):
    b_in = 2 * MAIN_WIDTH + MEM_WIDTH
    j = layer - N_A_LAYERS
    return pl.pallas_call(
        _inproj_b_kernel,
        grid=(TOKENS // ROW_TILE,),
        in_specs=[_row_spec(D_MODEL), _layer((1, D_MODEL), layer),
                  _layer((D_MODEL, b_in), j), _layer((1, MAIN_WIDTH), j),
                  _row_spec(MAIN_WIDTH)],
        out_specs=[_row_spec(2 * MAIN_WIDTH), _row_spec(MAIN_WIDTH), _row_spec(MEM_WIDTH)],
        out_shape=[jax.ShapeDtypeStruct((TOKENS, 2 * MAIN_WIDTH), BF16),
                   jax.ShapeDtypeStruct((TOKENS, MAIN_WIDTH), F32),
                   jax.ShapeDtypeStruct((TOKENS, MEM_WIDTH), F32)],
        compiler_params=_params(("parallel",)),
        name="inproj_b",
    )(x, gains, w_in, q_gain_tiled, q_bias)


def _hgrn_kernel(q_ref, lf_ref, v_ref, g_ref, gain_ref, o_ref,
                 cpad_ref, kpad_ref, vpad_ref):
    C, SB, HD = HG_CHUNK, HG_SUB, HG_HEAD_DIM
    n_sub = C // SB
    r_i = lax.broadcasted_iota(jnp.int32, (C, C), 0)
    c_i = lax.broadcasted_iota(jnp.int32, (C, C), 1)
    tril = (r_i >= c_i).astype(BF16)
    gain = gain_ref[...]

    def intra_exact(q, k, v16, c):
        outs = [jnp.zeros((SB, HD), F32)]
        for b in range(1, n_sub):
            lo = b * SB
            ref = c[lo - 1:lo, :]
            qb = (q[lo:lo + SB, :] * jnp.exp(c[lo:lo + SB, :] - ref)).astype(BF16)
            kb = (k[0:lo, :] * jnp.exp(ref - c[0:lo, :])).astype(BF16)
            p_b = lax.dot_general(qb, kb, (((1,), (1,)), ((), ())),
                                  preferred_element_type=F32)
            outs.append(jnp.dot(p_b.astype(BF16), v16[0:lo, :], preferred_element_type=F32))
        o = jnp.concatenate(outs, axis=0)
        ones = jnp.ones((HD, HD), BF16)
        row_in_block = lax.broadcasted_iota(jnp.int32, (C, HD), 0) % SB
        cpad_ref[SB:SB + C, :] = c
        kpad_ref[SB:SB + C, :] = k
        vpad_ref[SB:SB + C, :] = v16.astype(F32)
        for d in range(SB):
            cs = cpad_ref[SB - d:SB - d + C, :]
            ks = kpad_ref[SB - d:SB - d + C, :]
            vs = vpad_ref[SB - d:SB - d + C, :]
            wgt = jnp.where(row_in_block >= d, q * ks * jnp.exp(c - cs), 0.0)
            s = jnp.dot(wgt.astype(BF16), ones, preferred_element_type=F32)
            o = o + s * vs
        return o

    def chunk(n, st, intra):
        r0 = pl.multiple_of(n * C, C)
        lf = lf_ref[0, pl.ds(r0, C), :]
        q = q_ref[0, pl.ds(r0, C), :]
        v16 = v_ref[0, pl.ds(r0, C), :]
        k = 1.0 - jnp.exp(lf)
        c = None
        rem = lf
        for p in range(3):
            piece = rem.astype(BF16)
            term = jnp.dot(tril, piece, preferred_element_type=F32)
            c = term if c is None else c + term
            rem = rem - piece.astype(F32)
        c_end = c[C - 1:C, :]

        qe = (q * jnp.exp(c)).astype(BF16)
        o = lax.dot_general(qe, st.astype(BF16), (((1,), (1,)), ((), ())),
                            preferred_element_type=F32)
        kd = (k * jnp.exp(c_end - c)).astype(BF16)
        upd = lax.dot_general(v16, kd, (((0,), (0,)), ((), ())),
                              preferred_element_type=F32)
        o = o + intra(q, k, v16, c)
        o_ref[0, pl.ds(r0, C), :] = (_rms(o, gain) * g_ref[0, pl.ds(r0, C), :]).astype(BF16)
        return st * jnp.exp(c_end) + upd

    G = HG_GROUP
    R = G * C
    gr = lax.broadcasted_iota(jnp.int32, (R, R), 0)
    gc = lax.broadcasted_iota(jnp.int32, (R, R), 1)
    tril_group = ((gr // C == gc // C) & (gr >= gc)).astype(BF16)
    sub_of_row = lax.broadcasted_iota(jnp.int32, (C, HD), 0) // SB
    causal = r_i >= c_i

    def stage_cumsum(i):
        lf = lf_ref[0, i * R:(i + 1) * R, :]
        c = None
        rem = lf
        for p in range(3):
            piece = rem.astype(BF16)
            term = jnp.dot(tril_group, piece, preferred_element_type=F32)
            c = term if c is None else c + term
            rem = rem - piece.astype(F32)
        return lf, c

    def stage_scores(i, lf, c):
        q = q_ref[0, i * R:(i + 1) * R, :]
        v16 = v_ref[0, i * R:(i + 1) * R, :]
        k = 1.0 - jnp.exp(lf)
        lhs, rhs, c_ends = [], [], []
        for g in range(G):
            cg = c[g * C:(g + 1) * C, :]
            qg = q[g * C:(g + 1) * C, :]
            kg = k[g * C:(g + 1) * C, :]
            refs = [jnp.zeros((1, HD), F32)] + [cg[b * SB - 1:b * SB, :] for b in range(1, n_sub)]
            own_ref = jnp.concatenate([jnp.broadcast_to(r, (SB, HD)) for r in refs], axis=0)
            k_own = kg * jnp.exp(own_ref - cg)
            lhs.append(jnp.concatenate(
                [(qg * jnp.exp(jnp.minimum(cg - r, 0.0))).astype(BF16) for r in refs], axis=1))
            rhs.append(jnp.concatenate(
                [jnp.where(sub_of_row == b, k_own, 0.0).astype(BF16) for b in range(n_sub)],
                axis=1))
            c_ends.append(cg[C - 1:C, :])
        c_end_rows = jnp.concatenate([jnp.broadcast_to(e, (C, HD)) for e in c_ends], axis=0)
        qe = (q * jnp.exp(c)).astype(BF16)
        kd = (k * jnp.exp(c_end_rows - c)).astype(BF16)

        scores = [lax.dot_general(lhs[g], rhs[g], (((1,), (1,)), ((), ())),
                                  preferred_element_type=F32) for g in range(G)]
        upds = [lax.dot_general(v16[g * C:(g + 1) * C, :], kd[g * C:(g + 1) * C, :],
                                (((0,), (0,)), ((), ())), preferred_element_type=F32)
                for g in range(G)]
        return scores, upds, qe, c_ends

    def stage_output(i, prepared, st):
        scores, upds, qe, c_ends = prepared
        v16 = v_ref[0, i * R:(i + 1) * R, :]
        intra = [jnp.dot(jnp.where(causal, scores[g], 0.0).astype(BF16),
                         v16[g * C:(g + 1) * C, :], preferred_element_type=F32)
                 for g in range(G)]
        outs = []
        for g in range(G):
            inter = lax.dot_general(qe[g * C:(g + 1) * C, :], st.astype(BF16),
                                    (((1,), (1,)), ((), ())), preferred_element_type=F32)
            outs.append(inter + intra[g])
            st = st * jnp.exp(c_ends[g]) + upds[g]
        o = jnp.concatenate(outs, axis=0)
        rows = slice(i * R, (i + 1) * R)
        o_ref[0, rows, :] = (_rms(o, gain) * g_ref[0, rows, :]).astype(BF16)
        return st

    def run_factored():
        n_groups = SEQ // R
        st = jnp.zeros((HD, HD), F32)
        prepared = stage_scores(0, *stage_cumsum(0))
        for i in range(n_groups):
            if i + 1 < n_groups:
                ahead = stage_cumsum(i + 1)
            st = stage_output(i, prepared, st)
            if i + 1 < n_groups:
                prepared = stage_scores(i + 1, *ahead)

    def run_exact():
        def body(n, st):
            return chunk(n, st, intra_exact)
        lax.fori_loop(0, SEQ // C, body, jnp.zeros((HD, HD), F32))

    blk = lax.broadcasted_iota(jnp.int32, (SEQ // SB, SEQ), 0)
    tok = lax.broadcasted_iota(jnp.int32, (SEQ // SB, SEQ), 1) // SB
    block_decay = jnp.dot((blk == tok).astype(BF16), lf_ref[0].astype(BF16),
                          preferred_element_type=F32)
    mild = jnp.min(block_decay) >= -HG_SAFE_DECAY

    @pl.when(mild)
    def _():
        run_factored()

    @pl.when(jnp.logical_not(mild))
    def _():
        zpad = jnp.zeros((SB, HD), F32)
        cpad_ref[0:SB, :] = zpad
        kpad_ref[0:SB, :] = zpad
        vpad_ref[0:SB, :] = zpad
        run_exact()


def _hgrn(layer, q, lf, v, g, o_gains):
    spec = pl.BlockSpec((1, SEQ, HG_HEAD_DIM), lambda b, h: (b, 0, h))
    return pl.pallas_call(
        _hgrn_kernel,
        grid=(BATCH, HG_HEADS),
        in_specs=[spec, spec, spec, spec, _layer((1, HG_HEAD_DIM), layer)],
        out_specs=spec,
        out_shape=jax.ShapeDtypeStruct((BATCH, SEQ, MAIN_WIDTH), BF16),
        scratch_shapes=[pltpu.VMEM((HG_SUB + HG_CHUNK, HG_HEAD_DIM), F32),
                        pltpu.VMEM((HG_SUB + HG_CHUNK, HG_HEAD_DIM), F32),
                        pltpu.VMEM((HG_SUB + HG_CHUNK, HG_HEAD_DIM), F32)],
        compiler_params=_params(("parallel", "parallel")),
        name="hgrn",
    )(q, lf, v, g, o_gains)


FOX_BIAS_PIECES = 3
FOX_PIECE_STRIDE = 16
LOG2E = 1.4426950408889634


def _bias_slot_layout():
    lane = lax.broadcasted_iota(jnp.int32, (V7X_LANES, MAIN_WIDTH), 1)
    pair = lane // (2 * FOX_HEAD_DIM)
    half = (lane // FOX_HEAD_DIM) % 2
    head_here = 2 * pair + (1 - half)
    slot = lane % FOX_HEAD_DIM
    return head_here, slot


def _place_bias(packed):
    head_here, slot = _bias_slot_layout()
    src = lax.broadcasted_iota(jnp.int32, (V7X_LANES, MAIN_WIDTH), 0)
    src_piece, src_head = src // FOX_PIECE_STRIDE, src % FOX_PIECE_STRIDE
    hit = (src_piece < FOX_BIAS_PIECES) & (src_head == head_here)
    place = jnp.concatenate([(hit & (slot == src_piece)).astype(BF16),
                             (hit & (slot == src_piece + FOX_BIAS_PIECES)).astype(BF16)],
                            axis=1)
    return jnp.dot(packed, place, preferred_element_type=F32)


def _with_bias_lanes(pair_vals, pair_bias, head_in_pair):
    own = (lax.broadcasted_iota(jnp.int32, pair_vals.shape, 1) // FOX_HEAD_DIM) == head_in_pair
    return jnp.where(own, pair_vals, pair_bias)


def _kv_kernel(x_ref, gain_ref, wk_ref, wv_ref, wf_ref, fb_ref, kg_ref,
               ka_ref, vt_ref, qbias_ref, carry_ref, v_ref):
    @pl.when(pl.program_id(1) == 0)
    def _():
        carry_ref[...] = jnp.zeros_like(carry_ref)

    n = _rms(x_ref[0], gain_ref[...]).astype(BF16)

    z = _mm(n, wf_ref[...]) + fb_ref[...]
    v_ref[...] = _mm(n, wv_ref[...])
    k_raw = [_mm(n, wk_ref[:, c * 256:(c + 1) * 256]) for c in range(MAIN_WIDTH // 256)]
    log_f = jnp.minimum(z, 0.0) - jnp.log1p(jnp.exp(-jnp.abs(z)))
    seg = KV_CUM_SEG
    tril = (lax.broadcasted_iota(jnp.int32, (seg, seg), 0)
            >= lax.broadcasted_iota(jnp.int32, (seg, seg), 1)).astype(BF16)
    local = []
    for sgi in range(KV_TILE // seg):
        rem = log_f[sgi * seg:(sgi + 1) * seg, :]
        acc = None
        for p in range(3):
            piece = rem.astype(BF16)
            term = jnp.dot(tril, piece, preferred_element_type=F32)
            acc = term if acc is None else acc + term
            rem = rem - piece.astype(F32)
        local.append(acc)
    last = carry_ref[...]
    parts = []
    for acc in local:
        parts.append(acc + last)
        last = parts[-1][seg - 1:seg, :]
    carry_ref[...] = last
    cum = jnp.concatenate(parts, axis=0)

    lane = lax.broadcasted_iota(jnp.int32, cum.shape, 1)
    rem = jnp.where(lane < FOX_HEADS, cum * LOG2E, 0.0)
    packed = None
    for p in range(FOX_BIAS_PIECES):
        piece = rem.astype(BF16).astype(F32)
        moved = pltpu.roll(piece, p * FOX_PIECE_STRIDE, axis=1) if p else piece
        packed = moved if packed is None else packed + moved
        rem = rem - piece
    placed = _place_bias(packed.astype(BF16))
    _, slot = _bias_slot_layout()
    slot_row = slot[0:1, :]
    n_p = FOX_BIAS_PIECES
    k_bias = (placed[:, :MAIN_WIDTH]
              + ((slot_row >= n_p) & (slot_row < 2 * n_p)).astype(F32)).astype(BF16)
    qbias_ref[0] = (placed[:, MAIN_WIDTH:] - (slot_row < n_p).astype(F32)).astype(BF16)

    pair_w = 2 * FOX_HEAD_DIM
    for c in range(MAIN_WIDTH // 256):
        sl = slice(c * 256, (c + 1) * 256)
        k_n = _group_rms(k_raw[c], kg_ref[:, sl], FOX_HEAD_DIM).astype(BF16)
        for pp in range(256 // pair_w):
            lo = c * 256 + pp * pair_w
            for hh in range(2):
                dst = 2 * lo + hh * pair_w
                ka_ref[0, :, dst:dst + pair_w] = _with_bias_lanes(
                    k_n[:, pp * pair_w:(pp + 1) * pair_w], k_bias[:, lo:lo + pair_w], hh)
    vt_ref[0] = v_ref[...].T.astype(BF16)


def _kv(x, gain, w_kv, wf, f_bias, k_gain_tiled):
    row = lambda w: pl.BlockSpec((1, KV_TILE, w), lambda b, i: (b, i, 0))
    return pl.pallas_call(
        _kv_kernel,
        grid=(BATCH, SEQ // KV_TILE),
        in_specs=[row(D_MODEL), _resident((1, D_MODEL)),
                  _resident((D_MODEL, MAIN_WIDTH), (0, 0)),
                  _resident((D_MODEL, MAIN_WIDTH), (0, 1)), _resident((D_MODEL, V7X_LANES)),
                  _resident((1, V7X_LANES)), _resident((1, MAIN_WIDTH))],
        out_specs=[row(2 * MAIN_WIDTH),
                   pl.BlockSpec((1, MAIN_WIDTH, KV_TILE), lambda b, i: (b, 0, i)),
                   row(MAIN_WIDTH)],
        out_shape=[jax.ShapeDtypeStruct((BATCH, SEQ, 2 * MAIN_WIDTH), BF16),
                   jax.ShapeDtypeStruct((BATCH, MAIN_WIDTH, SEQ), BF16),
                   jax.ShapeDtypeStruct((BATCH, SEQ, MAIN_WIDTH), BF16)],
        scratch_shapes=[pltpu.VMEM((1, V7X_LANES), F32),
                        pltpu.VMEM((KV_TILE, MAIN_WIDTH), F32)],
        compiler_params=_params(("parallel", "arbitrary")),
        name="kv",
    )(x, gain, w_kv, w_kv, wf, f_bias, k_gain_tiled)


def _fox_kernel(qa_ref, ka_ref, vt_ref, gate_ref, o_ref, s_ref, m_ref, acc_ref):
    T, HD = FOX_TILE, FOX_HEAD_DIM
    W = 2 * HD
    H = T // 2

    def pieces(i, j):
        if j == i:
            return [(slice(0, H), slice(0, T)), (slice(H, T), slice(H, T))]
        return [(slice(0, T), slice(0, T))]

    def issue_scores(slot, i, j):
        for hh in range(2):
            for keys, cols in pieces(i, j):
                s_ref[slot, hh, keys, cols] = lax.dot_general(
                    ka_ref[0, j * T + keys.start:j * T + keys.stop, hh * W:(hh + 1) * W],
                    qa_ref[0, i * T + cols.start:i * T + cols.stop, hh * W:(hh + 1) * W],
                    (((1,), (1,)), ((), ())), preferred_element_type=F32)

    def consume(slot, i, j):
        for n_piece, (keys, cols) in enumerate(pieces(i, j)):
            n_k, n_q = keys.stop - keys.start, cols.stop - cols.start
            first = j == 0 and n_piece == 0
            probs, alphas = [], []
            for hh in range(2):
                s = s_ref[slot, hh, keys, cols]
                if j == i:
                    key_pos = lax.broadcasted_iota(jnp.int32, (n_k, n_q), 0) + keys.start
                    query_pos = lax.broadcasted_iota(jnp.int32, (n_k, n_q), 1) + cols.start
                    s = jnp.where(key_pos <= query_pos, s, -jnp.inf)
                m_new = jnp.max(s, axis=0, keepdims=True)
                if not first:
                    m_old = m_ref[hh, :, cols]
                    m_new = jnp.maximum(m_old, m_new)
                    alphas.append(jnp.exp2(m_old - m_new))
                probs.append(jnp.exp2(s - m_new).astype(BF16))
                m_ref[hh, :, cols] = m_new
            for hh in range(2):
                vt = jnp.concatenate(
                    [vt_ref[0, hh * HD:(hh + 1) * HD, j * T + keys.start:j * T + keys.stop],
                     jnp.ones((ONES_ROWS, n_k), BF16)], axis=0)
                pv = jnp.dot(vt, probs[hh], preferred_element_type=F32)
                acc_ref[hh, :, cols] = pv if first else alphas[hh] * acc_ref[hh, :, cols] + pv
        if j == i:
            out_t = jnp.concatenate([acc_ref[hh, 0:HD, :] / acc_ref[hh, HD:HD + 1, :]
                                     for hh in range(2)], axis=0)
            rows = slice(i * T, (i + 1) * T)
            o_ref[0, rows, :] = (out_t.T * gate_ref[0, rows, :]).astype(BF16)

    blocks = [(i, j) for i in range(SEQ // T) for j in range(i + 1)]
    issue_scores(0, *blocks[0])
    for n, (i, j) in enumerate(blocks):
        if n + 1 < len(blocks):
            issue_scores((n + 1) % 2, *blocks[n + 1])
        consume(n % 2, i, j)


def _fox(qa, ka, vt, gate):
    T = FOX_TILE
    W = 2 * FOX_HEAD_DIM
    seq_spec = lambda w: pl.BlockSpec((1, SEQ, w), lambda b, p: (b, 0, p))
    return pl.pallas_call(
        _fox_kernel,
        grid=(BATCH, FOX_HEADS // 2),
        in_specs=[seq_spec(2 * W), seq_spec(2 * W),
                  pl.BlockSpec((1, W, SEQ), lambda b, p: (b, p, 0)),
                  seq_spec(W)],
        out_specs=seq_spec(W),
        out_shape=jax.ShapeDtypeStruct((BATCH, SEQ, MAIN_WIDTH), BF16),
        scratch_shapes=[pltpu.VMEM((2, 2, T, T), F32),
                        pltpu.VMEM((2, 1, T), F32),
                        pltpu.VMEM((2, FOX_HEAD_DIM + ONES_ROWS, T), F32)],
        compiler_params=_params(("parallel", "parallel")),
        name="fox",
    )(qa, ka, vt, gate)


def _memattn_kernel(qm_ref, mem_ref, mgain_ref, wkv_ref, qg_ref, kg_ref, o_ref,
                    kv_ref, km_ref, vmt_ref):
    W, HD = MEM_WIDTH, MEM_HEAD_DIM

    @pl.when(pl.program_id(1) == 0)
    def _():
        mem_n = _rms(mem_ref[0], mgain_ref[...]).astype(BF16)
        kv_ref[...] = _mm(mem_n, wkv_ref[...])
        km = _group_rms(kv_ref[:, :W], kg_ref[...], HD).astype(BF16)
        head_of_lane = lax.broadcasted_iota(jnp.int32, km.shape, 1) // HD
        vm_t = kv_ref[:, W:].T.astype(BF16)
        ones_rows = jnp.ones((ONES_ROWS, N_MEM), BF16)
        for h in range(MEM_HEADS):
            km_ref[h] = jnp.where(head_of_lane == h, km, jnp.zeros_like(km))
            vmt_ref[h] = jnp.concatenate([vm_t[h * HD:(h + 1) * HD, :], ones_rows], axis=0)

    scale = HD ** -0.5 * LOG2E
    qn = (_group_rms(qm_ref[0], qg_ref[...], HD) * scale).astype(BF16)
    scores = [lax.dot_general(km_ref[h], qn, (((1,), (1,)), ((), ())),
                              preferred_element_type=F32) for h in range(MEM_HEADS)]
    outs = []
    for h in range(MEM_HEADS):
        s = scores[h]
        p = jnp.exp2(s - jnp.max(s, axis=0, keepdims=True)).astype(BF16)
        r = jnp.dot(vmt_ref[h], p, preferred_element_type=F32)
        outs.append(r[0:HD, :] / r[HD:HD + 1, :])
    o_ref[0] = jnp.concatenate(outs, axis=0).T.astype(BF16)


def _memattn(layer, qm, mem, mem_gains, w_mem_kv, q_gains_tiled, k_gains_tiled):
    T = MEM_Q_TILE
    qspec = pl.BlockSpec((1, T, MEM_WIDTH), lambda b, i: (b, i, 0))
    return pl.pallas_call(
        _memattn_kernel,
        grid=(BATCH, SEQ // T),
        in_specs=[qspec,
                  pl.BlockSpec((1, N_MEM, D_MODEL), lambda b, i: (b, 0, 0)),
                  _layer((1, D_MODEL), layer), _layer((D_MODEL, 2 * MEM_WIDTH), layer),
                  _layer((1, MEM_WIDTH), layer), _layer((1, MEM_WIDTH), layer)],
        out_specs=qspec,
        out_shape=jax.ShapeDtypeStruct((BATCH, SEQ, MEM_WIDTH), BF16),
        scratch_shapes=[pltpu.VMEM((N_MEM, 2 * MEM_WIDTH), F32),
                        pltpu.VMEM((MEM_HEADS, N_MEM, MEM_WIDTH), BF16),
                        pltpu.VMEM((MEM_HEADS, MEM_HEAD_DIM + ONES_ROWS, N_MEM), BF16)],
        compiler_params=_params(("parallel", "arbitrary")),
        name="memattn",
    )(qm, mem, mem_gains, w_mem_kv, q_gains_tiled, k_gains_tiled)


def kernel(x, mem, ffn1_norm, ffn1_w_gate, ffn1_w_up, ffn1_w_down, mix_norm, mem_norm,
           w_mem_kv, mem_q_gain, mem_k_gain, w_in_a, hgrn_lb_logits, hgrn_o_gain,
           w_in_b, fox_q_gain, kv_norm, w_kv, fox_f_bias, fox_k_gain, w_out,
           ffn2_norm, ffn2_w_gate, ffn2_w_up, ffn2_w_down):
    row = lambda t: t.reshape(1, -1).astype(F32)
    rows = lambda t, reps=1: jnp.tile(t.astype(F32), (1, reps)).reshape(t.shape[0], 1, -1)
    f32 = lambda t: t.astype(F32)
    x = x.reshape(TOKENS, D_MODEL).astype(F32)
    mem = mem.astype(F32)
    lb_logits = hgrn_lb_logits.astype(F32)
    ffn1 = (rows(ffn1_norm), f32(ffn1_w_gate), f32(ffn1_w_up), f32(ffn1_w_down))
    ffn2 = (rows(ffn2_norm), f32(ffn2_w_gate), f32(ffn2_w_up), f32(ffn2_w_down))
    mix_gains = rows(mix_norm)
    o_gains, fox_q_gains = rows(hgrn_o_gain), rows(fox_q_gain, FOX_HEADS)
    mem_args = (rows(mem_norm), f32(w_mem_kv), rows(mem_q_gain, MEM_HEADS),
                rows(mem_k_gain, MEM_HEADS))

    for l in range(DEPTH):
        x = _ffn(l, x, *ffn1)
        if l < N_A_LAYERS:
            qs, lf, iv, sg, qm = _inproj_a(l, x, mix_gains, f32(w_in_a), lb_logits)
            to3 = lambda t: t.reshape(BATCH, SEQ, MAIN_WIDTH)
            main = _hgrn(l, to3(qs), to3(lf), to3(iv), to3(sg), o_gains)
        else:
            qa, gate, qm = _inproj_b(l, x, mix_gains, f32(w_in_b), fox_q_gains,
                                     q_bias.reshape(TOKENS, MAIN_WIDTH))
            main = _fox(qa.reshape(BATCH, SEQ, 2 * MAIN_WIDTH), k_aug, v_t,
                        gate.reshape(BATCH, SEQ, MAIN_WIDTH))
        memo = _memattn(l, qm.reshape(BATCH, SEQ, MEM_WIDTH), mem, *mem_args)
        x = _mix_ffn(l, x, main.reshape(TOKENS, MAIN_WIDTH), memo.reshape(TOKENS, MEM_WIDTH),
                     f32(w_out), *ffn2)
        if l == N_A_LAYERS - 1:
            pad = V7X_LANES - FOX_HEADS
            wf = jnp.pad(f32(w_kv[:, 2 * MAIN_WIDTH:]), ((0, 0), (0, pad)))
            fb = jnp.pad(fox_f_bias.astype(F32), (0, pad)).reshape(1, -1)
            k_aug, v_t, q_bias = _kv(x.reshape(BATCH, SEQ, D_MODEL), row(kv_norm),
                                     f32(w_kv), wf, fb,
                                     row(jnp.tile(fox_k_gain, FOX_HEADS)))
    return x.reshape(BATCH, SEQ, D_MODEL)
```

```python
import functools

import jax
import jax.numpy as jnp
from jax import lax
from jax.experimental import pallas as pl
from jax.experimental.pallas import tpu as pltpu

F32 = jnp.float32
BF16 = jnp.bfloat16

D_MODEL = 1024
BATCH = 8
SEQ = 2048
DEPTH = 4
N_MEM = 256
N_A_LAYERS = DEPTH // 2
MAIN_WIDTH = 768
MEM_WIDTH = 256
HG_HEAD_DIM = 128
HG_HEADS = MAIN_WIDTH // HG_HEAD_DIM
FOX_HEAD_DIM = 64
FOX_HEADS = MAIN_WIDTH // FOX_HEAD_DIM
MEM_HEADS = 4
MEM_HEAD_DIM = MEM_WIDTH // MEM_HEADS
D_FF = 2816
EPS = 1e-6
TOKENS = BATCH * SEQ

V7X_LANES = 128
V7X_VMEM_BYTES = 64 * 1024 * 1024

ROW_TILE = 512
FF_CHUNK = 256
HG_CHUNK = 64
HG_SUB = 16
HG_SAFE_DECAY = 60.0
HG_GROUP = 4
HG_HEADS_PER_STEP = 2
FOX_TILE = 512
ONES_ROWS = 16
MEM_BATCH = 2
KV_TILE = 512
KV_CUM_SEG = 256
VMEM_LIMIT = 56 * 1024 * 1024


def _rms(x, gain):
    ms = jnp.mean(x * x, axis=-1, keepdims=True)
    return x * lax.rsqrt(ms + EPS) * gain


def _mm(a, w):
    return jnp.dot(a, w.astype(BF16), preferred_element_type=F32)


def _split_dot(x, rhs01, parts):
    acc = None
    rem = x
    for p in range(parts):
        piece = rem.astype(BF16)
        term = jnp.dot(piece, rhs01, preferred_element_type=F32)
        acc = term if acc is None else acc + term
        if p + 1 < parts:
            rem = rem - piece.astype(F32)
    return acc


def _group_ones(width, group):
    r = lax.broadcasted_iota(jnp.int32, (width, width), 0) // group
    c = lax.broadcasted_iota(jnp.int32, (width, width), 1) // group
    return (r == c).astype(BF16)


def _group_rms(x, gain_tiled, group):
    width = x.shape[-1]
    ss = _split_dot(x * x, _group_ones(width, group), 2)
    return x * lax.rsqrt(ss * (1.0 / group) + EPS) * gain_tiled


def _resident(shape, index=None):
    index = (0,) * len(shape) if index is None else index
    return pl.BlockSpec(shape, lambda *_: index, pipeline_mode=pl.Buffered(1))


def _layer(shape, layer, first=0):
    index = (layer, first) + (0,) * (len(shape) - 1)
    return pl.BlockSpec((None,) + tuple(shape), lambda *_: index,
                        pipeline_mode=pl.Buffered(1))


def _params(sem):
    return pltpu.CompilerParams(dimension_semantics=sem, vmem_limit_bytes=VMEM_LIMIT)


def _swiglu_rows(x, gain_ref, wg_ref, wu_ref, wd_ref, h_ref):
    n = _rms(x, gain_ref[...]).astype(BF16)
    for c in range(D_FF // FF_CHUNK):
        sl = slice(c * FF_CHUNK, (c + 1) * FF_CHUNK)
        g = _mm(n, wg_ref[:, sl])
        u = _mm(n, wu_ref[:, sl])
        h_ref[:, sl] = (g * jax.nn.sigmoid(g) * u).astype(BF16)
    y = _mm(h_ref[...], wd_ref[...])
    return x + 0.5 * y


def _ffn_kernel(x_ref, gain_ref, wg_ref, wu_ref, wd_ref, o_ref, h_ref):
    o_ref[...] = _swiglu_rows(x_ref[...], gain_ref, wg_ref, wu_ref, wd_ref, h_ref)


def _mix_ffn_kernel(x_ref, main_ref, memo_ref, wo_main_ref, wo_mem_ref,
                    gain_ref, wg_ref, wu_ref, wd_ref, o_ref, h_ref):
    x = (x_ref[...]
         + _mm(main_ref[...], wo_main_ref[...])
         + _mm(memo_ref[...], wo_mem_ref[...]))
    o_ref[...] = _swiglu_rows(x, gain_ref, wg_ref, wu_ref, wd_ref, h_ref)


def _row_spec(width):
    return pl.BlockSpec((ROW_TILE, width), lambda i: (i, 0))


def _ffn_weight_specs(layer):
    return [_layer((1, D_MODEL), layer), _layer((D_MODEL, D_FF), layer),
            _layer((D_MODEL, D_FF), layer), _layer((D_FF, D_MODEL), layer)]


def _ffn(layer, x, gains, wg, wu, wd):
    return pl.pallas_call(
        _ffn_kernel,
        grid=(TOKENS // ROW_TILE,),
        in_specs=[_row_spec(D_MODEL)] + _ffn_weight_specs(layer),
        out_specs=_row_spec(D_MODEL),
        out_shape=jax.ShapeDtypeStruct((TOKENS, D_MODEL), F32),
        scratch_shapes=[pltpu.VMEM((ROW_TILE, D_FF), BF16)],
        compiler_params=_params(("parallel",)),
        name="ffn",
    )(x, gains, wg, wu, wd)


def _mix_ffn(layer, x, main, memo, w_out, gains, wg, wu, wd):
    assert MAIN_WIDTH % MEM_WIDTH == 0
    return pl.pallas_call(
        _mix_ffn_kernel,
        grid=(TOKENS // ROW_TILE,),
        in_specs=[_row_spec(D_MODEL), _row_spec(MAIN_WIDTH), _row_spec(MEM_WIDTH),
                  _layer((MAIN_WIDTH, D_MODEL), layer),
                  _layer((MEM_WIDTH, D_MODEL), layer, MAIN_WIDTH // MEM_WIDTH)]
        + _ffn_weight_specs(layer),
        out_specs=_row_spec(D_MODEL),
        out_shape=jax.ShapeDtypeStruct((TOKENS, D_MODEL), F32),
        scratch_shapes=[pltpu.VMEM((ROW_TILE, D_FF), BF16)],
        compiler_params=_params(("parallel",)),
        name="mix_ffn",
    )(x, main, memo, w_out, w_out, gains, wg, wu, wd)


def _inproj_a_kernel(layer, x_ref, gain_ref, w_ref, lbl_ref,
                     q_ref, lf_ref, i_ref, g_ref, qm_ref):
    n = _rms(x_ref[...], gain_ref[...]).astype(BF16)
    rows = [lbl_ref[r:r + 1, :] for r in range(N_A_LAYERS)]
    top = functools.reduce(jnp.maximum, rows)
    exps = [jnp.exp(r - top) for r in rows]
    total = functools.reduce(jnp.add, exps)
    lb = jnp.zeros_like(top)
    for r in range(1, layer + 1):
        lb = lb + exps[r] / total

    w = MAIN_WIDTH
    q_raw = _mm(n, w_ref[:, 0:w])
    q_ref[...] = q_raw * jax.nn.sigmoid(q_raw)
    f_raw = _mm(n, w_ref[:, w:2 * w])
    lf_ref[...] = jnp.log(lb + (1.0 - lb) * jax.nn.sigmoid(f_raw))
    i_ref[...] = _mm(n, w_ref[:, 2 * w:3 * w]).astype(BF16)
    g_raw = _mm(n, w_ref[:, 3 * w:4 * w])
    g_ref[...] = g_raw * jax.nn.sigmoid(g_raw)
    qm_ref[...] = _mm(n, w_ref[:, 4 * w:])


def _inproj_a(layer, x, gains, w_in, lb_logits):
    a_in = 4 * MAIN_WIDTH + MEM_WIDTH
    return pl.pallas_call(
        functools.partial(_inproj_a_kernel, layer),
        grid=(TOKENS // ROW_TILE,),
        in_specs=[_row_spec(D_MODEL), _layer((1, D_MODEL), layer),
                  _layer((D_MODEL, a_in), layer), _resident((N_A_LAYERS, MAIN_WIDTH))],
        out_specs=[_row_spec(MAIN_WIDTH), _row_spec(MAIN_WIDTH), _row_spec(MAIN_WIDTH),
                   _row_spec(MAIN_WIDTH), _row_spec(MEM_WIDTH)],
        out_shape=[jax.ShapeDtypeStruct((TOKENS, MAIN_WIDTH), F32),
                   jax.ShapeDtypeStruct((TOKENS, MAIN_WIDTH), F32),
                   jax.ShapeDtypeStruct((TOKENS, MAIN_WIDTH), BF16),
                   jax.ShapeDtypeStruct((TOKENS, MAIN_WIDTH), F32),
                   jax.ShapeDtypeStruct((TOKENS, MEM_WIDTH), F32)],
        compiler_params=_params(("parallel",)),
        name="inproj_a",
    )(x, gains, w_in, lb_logits)


def _inproj_b_kernel(x_ref, gain_ref, w_ref, qg_ref, qbias_ref, qa_ref, gate_ref, qm_ref):
    n = _rms(x_ref[...], gain_ref[...]).astype(BF16)
    w = MAIN_WIDTH
    scale = FOX_HEAD_DIM ** -0.5 * LOG2E
    pair_w = 2 * FOX_HEAD_DIM
    q_raw = [_mm(n, w_ref[:, c * 256:(c + 1) * 256]) for c in range(w // 256)]
    gate = _mm(n, w_ref[:, w:2 * w])
    qm_ref[...] = _mm(n, w_ref[:, 2 * w:])
    for c in range(w // 256):
        sl = slice(c * 256, (c + 1) * 256)
        q_n = (_group_rms(q_raw[c], qg_ref[:, sl], FOX_HEAD_DIM) * scale).astype(BF16)
        for pp in range(256 // pair_w):
            lo = c * 256 + pp * pair_w
            for hh in range(2):
                dst = 2 * lo + hh * pair_w
                qa_ref[:, dst:dst + pair_w] = _with_bias_lanes(
                    q_n[:, pp * pair_w:(pp + 1) * pair_w], qbias_ref[:, lo:lo + pair_w], hh)
    gate_ref[...] = jax.nn.sigmoid(gate)


def _inproj_b(layer, x, gains, w_in, q_gain_tiled, q_bias):
    b_in = 2 * MAIN_WIDTH + MEM_WIDTH
    j = layer - N_A_LAYERS
    return pl.pallas_call(
        _inproj_b_kernel,
        grid=(TOKENS // ROW_TILE,),
        in_specs=[_row_spec(D_MODEL), _layer((1, D_MODEL), layer),
                  _layer((D_MODEL, b_in), j), _layer((1, MAIN_WIDTH), j),
                  _row_spec(MAIN_WIDTH)],
        out_specs=[_row_spec(2 * MAIN_WIDTH), _row_spec(MAIN_WIDTH), _row_spec(MEM_WIDTH)],
        out_shape=[jax.ShapeDtypeStruct((TOKENS, 2 * MAIN_WIDTH), BF16),
                   jax.ShapeDtypeStruct((TOKENS, MAIN_WIDTH), F32),
                   jax.ShapeDtypeStruct((TOKENS, MEM_WIDTH), F32)],
        compiler_params=_params(("parallel",)),
        name="inproj_b",
    )(x, gains, w_in, q_gain_tiled, q_bias)


def _hgrn_kernel(q_ref, lf_ref, v_ref, g_ref, gain_ref, o_ref,
                 cpad_ref, kpad_ref, vpad_ref):
    C, SB, HD = HG_CHUNK, HG_SUB, HG_HEAD_DIM
    n_sub = C // SB
    r_i = lax.broadcasted_iota(jnp.int32, (C, C), 0)
    c_i = lax.broadcasted_iota(jnp.int32, (C, C), 1)
    tril = (r_i >= c_i).astype(BF16)
    gain = gain_ref[...]

    def intra_exact(q, k, v16, c):
        outs = [jnp.zeros((SB, HD), F32)]
        for b in range(1, n_sub):
            lo = b * SB
            ref = c[lo - 1:lo, :]
            qb = (q[lo:lo + SB, :] * jnp.exp(c[lo:lo + SB, :] - ref)).astype(BF16)
            kb = (k[0:lo, :] * jnp.exp(ref - c[0:lo, :])).astype(BF16)
            p_b = lax.dot_general(qb, kb, (((1,), (1,)), ((), ())),
                                  preferred_element_type=F32)
            outs.append(jnp.dot(p_b.astype(BF16), v16[0:lo, :], preferred_element_type=F32))
        o = jnp.concatenate(outs, axis=0)
        ones = jnp.ones((HD, HD), BF16)
        row_in_block = lax.broadcasted_iota(jnp.int32, (C, HD), 0) % SB
        cpad_ref[SB:SB + C, :] = c
        kpad_ref[SB:SB + C, :] = k
        vpad_ref[SB:SB + C, :] = v16.astype(F32)
        for d in range(SB):
            cs = cpad_ref[SB - d:SB - d + C, :]
            ks = kpad_ref[SB - d:SB - d + C, :]
            vs = vpad_ref[SB - d:SB - d + C, :]
            wgt = jnp.where(row_in_block >= d, q * ks * jnp.exp(c - cs), 0.0)
            s = jnp.dot(wgt.astype(BF16), ones, preferred_element_type=F32)
            o = o + s * vs
        return o

    def chunk(n, st, intra, lanes):
        r0 = pl.multiple_of(n * C, C)
        lf = lf_ref[0, pl.ds(r0, C), lanes]
        q = q_ref[0, pl.ds(r0, C), lanes]
        v16 = v_ref[0, pl.ds(r0, C), lanes]
        k = 1.0 - jnp.exp(lf)
        c = None
        rem = lf
        for p in range(3):
            piece = rem.astype(BF16)
            term = jnp.dot(tril, piece, preferred_element_type=F32)
            c = term if c is None else c + term
            rem = rem - piece.astype(F32)
        c_end = c[C - 1:C, :]

        qe = (q * jnp.exp(c)).astype(BF16)
        o = lax.dot_general(qe, st.astype(BF16), (((1,), (1,)), ((), ())),
                            preferred_element_type=F32)
        kd = (k * jnp.exp(c_end - c)).astype(BF16)
        upd = lax.dot_general(v16, kd, (((0,), (0,)), ((), ())),
                              preferred_element_type=F32)
        o = o + intra(q, k, v16, c)
        o_ref[0, pl.ds(r0, C), lanes] = (
            _rms(o, gain) * g_ref[0, pl.ds(r0, C), lanes]).astype(BF16)
        return st * jnp.exp(c_end) + upd

    G = HG_GROUP
    R = G * C
    gr = lax.broadcasted_iota(jnp.int32, (R, R), 0)
    gc = lax.broadcasted_iota(jnp.int32, (R, R), 1)
    tril_group = ((gr // C == gc // C) & (gr >= gc)).astype(BF16)
    causal = r_i >= c_i

    def stage_cumsum(i):
        lf = lf_ref[0, i * R:(i + 1) * R, :]
        c = None
        rem = lf
        for p in range(3):
            piece = rem.astype(BF16)
            term = jnp.dot(tril_group, piece, preferred_element_type=F32)
            c = term if c is None else c + term
            rem = rem - piece.astype(F32)
        return lf, c

    def stage_scores(i, lanes, lf, c):
        lf, c = lf[:, lanes], c[:, lanes]
        q = q_ref[0, i * R:(i + 1) * R, lanes]
        v16 = v_ref[0, i * R:(i + 1) * R, lanes]
        k = 1.0 - jnp.exp(lf)
        lhs, rhs, c_ends = [], [], []
        for g in range(G):
            cg = c[g * C:(g + 1) * C, :]
            qg = q[g * C:(g + 1) * C, :]
            kg = k[g * C:(g + 1) * C, :]
            refs = [jnp.zeros((1, HD), F32)] + [cg[b * SB - 1:b * SB, :] for b in range(1, n_sub)]
            own_ref = jnp.concatenate([jnp.broadcast_to(r, (SB, HD)) for r in refs], axis=0)
            k_own = (kg * jnp.exp(own_ref - cg)).astype(BF16)
            lhs_cols, rhs_cols = [], []
            for b in range(n_sub):
                lo = b * SB
                q_b = (qg[lo:, :] * jnp.exp(cg[lo:, :] - refs[b])).astype(BF16)
                k_b = k_own[lo:lo + SB, :]
                above = [jnp.zeros((lo, HD), BF16)] if lo else []
                below = [jnp.zeros((C - lo - SB, HD), BF16)] if lo + SB < C else []
                lhs_cols.append(jnp.concatenate(above + [q_b], axis=0))
                rhs_cols.append(jnp.concatenate(above + [k_b] + below, axis=0))
            lhs.append(jnp.concatenate(lhs_cols, axis=1))
            rhs.append(jnp.concatenate(rhs_cols, axis=1))
            c_ends.append(cg[C - 1:C, :])
        c_end_rows = jnp.concatenate([jnp.broadcast_to(e, (C, HD)) for e in c_ends], axis=0)
        qe = (q * jnp.exp(c)).astype(BF16)
        kd = (k * jnp.exp(c_end_rows - c)).astype(BF16)

        scores = [lax.dot_general(lhs[g], rhs[g], (((1,), (1,)), ((), ())),
                                  preferred_element_type=F32) for g in range(G)]
        upds = [lax.dot_general(v16[g * C:(g + 1) * C, :], kd[g * C:(g + 1) * C, :],
                                (((0,), (0,)), ((), ())), preferred_element_type=F32)
                for g in range(G)]
        return scores, upds, qe, c_ends

    def stage_output(i, lanes, prepared, st):
        scores, upds, qe, c_ends = prepared
        v16 = v_ref[0, i * R:(i + 1) * R, lanes]
        intra = [jnp.dot(jnp.where(causal, scores[g], 0.0).astype(BF16),
                         v16[g * C:(g + 1) * C, :], preferred_element_type=F32)
                 for g in range(G)]
        outs = []
        for g in range(G):
            inter = lax.dot_general(qe[g * C:(g + 1) * C, :], st.astype(BF16),
                                    (((1,), (1,)), ((), ())), preferred_element_type=F32)
            outs.append(inter + intra[g])
            st = st * jnp.exp(c_ends[g]) + upds[g]
        o = jnp.concatenate(outs, axis=0)
        rows = slice(i * R, (i + 1) * R)
        o_ref[0, rows, lanes] = (_rms(o, gain) * g_ref[0, rows, lanes]).astype(BF16)
        return st

    head_lanes = [slice(hh * HD, (hh + 1) * HD) for hh in range(HG_HEADS_PER_STEP)]

    def run_factored():
        n_groups = SEQ // R
        first = stage_cumsum(0)
        prepared = [stage_scores(0, lanes, *first) for lanes in head_lanes]
        states = [jnp.zeros((HD, HD), F32) for _ in head_lanes]
        for i in range(n_groups):
            if i + 1 < n_groups:
                ahead = stage_cumsum(i + 1)
            states = [stage_output(i, lanes, prepared[hh], states[hh])
                      for hh, lanes in enumerate(head_lanes)]
            if i + 1 < n_groups:
                prepared = [stage_scores(i + 1, lanes, *ahead) for lanes in head_lanes]

    def run_exact():
        for lanes in head_lanes:
            def body(n, st, lanes=lanes):
                return chunk(n, st, intra_exact, lanes)
            lax.fori_loop(0, SEQ // C, body, jnp.zeros((HD, HD), F32))

    blk = lax.broadcasted_iota(jnp.int32, (SEQ // SB, SEQ), 0)
    tok = lax.broadcasted_iota(jnp.int32, (SEQ // SB, SEQ), 1) // SB
    block_decay = jnp.dot((blk == tok).astype(BF16), lf_ref[0].astype(BF16),
                          preferred_element_type=F32)
    mild = jnp.min(block_decay) >= -HG_SAFE_DECAY

    @pl.when(mild)
    def _():
        run_factored()

    @pl.when(jnp.logical_not(mild))
    def _():
        zpad = jnp.zeros((SB, HD), F32)
        cpad_ref[0:SB, :] = zpad
        kpad_ref[0:SB, :] = zpad
        vpad_ref[0:SB, :] = zpad
        run_exact()


def _hgrn(layer, q, lf, v, g, o_gains):
    spec = pl.BlockSpec((1, SEQ, HG_HEADS_PER_STEP * HG_HEAD_DIM), lambda b, h: (b, 0, h))
    return pl.pallas_call(
        _hgrn_kernel,
        grid=(BATCH, HG_HEADS // HG_HEADS_PER_STEP),
        in_specs=[spec, spec, spec, spec, _layer((1, HG_HEAD_DIM), layer)],
        out_specs=spec,
        out_shape=jax.ShapeDtypeStruct((BATCH, SEQ, MAIN_WIDTH), BF16),
        scratch_shapes=[pltpu.VMEM((HG_SUB + HG_CHUNK, HG_HEAD_DIM), F32),
                        pltpu.VMEM((HG_SUB + HG_CHUNK, HG_HEAD_DIM), F32),
                        pltpu.VMEM((HG_SUB + HG_CHUNK, HG_HEAD_DIM), F32)],
        compiler_params=_params(("parallel", "parallel")),
        name="hgrn",
    )(q, lf, v, g, o_gains)


FOX_BIAS_PIECES = 3
FOX_PIECE_STRIDE = 16
LOG2E = 1.4426950408889634


def _bias_slot_layout():
    lane = lax.broadcasted_iota(jnp.int32, (V7X_LANES, MAIN_WIDTH), 1)
    pair = lane // (2 * FOX_HEAD_DIM)
    half = (lane // FOX_HEAD_DIM) % 2
    head_here = 2 * pair + (1 - half)
    slot = lane % FOX_HEAD_DIM
    return head_here, slot


def _place_bias(packed):
    head_here, slot = _bias_slot_layout()
    src = lax.broadcasted_iota(jnp.int32, (V7X_LANES, MAIN_WIDTH), 0)
    src_piece, src_head = src // FOX_PIECE_STRIDE, src % FOX_PIECE_STRIDE
    hit = (src_piece < FOX_BIAS_PIECES) & (src_head == head_here)
    place = jnp.concatenate([(hit & (slot == src_piece)).astype(BF16),
                             (hit & (slot == src_piece + FOX_BIAS_PIECES)).astype(BF16)],
                            axis=1)
    return jnp.dot(packed, place, preferred_element_type=F32)


def _with_bias_lanes(pair_vals, pair_bias, head_in_pair):
    own = (lax.broadcasted_iota(jnp.int32, pair_vals.shape, 1) // FOX_HEAD_DIM) == head_in_pair
    return jnp.where(own, pair_vals, pair_bias)


def _kv_kernel(x_ref, gain_ref, wk_ref, wv_ref, wf_ref, fb_ref, kg_ref,
               ka_ref, vt_ref, qbias_ref, carry_ref, v_ref):
    @pl.when(pl.program_id(1) == 0)
    def _():
        carry_ref[...] = jnp.zeros_like(carry_ref)

    n = _rms(x_ref[0], gain_ref[...]).astype(BF16)

    z = _mm(n, wf_ref[...]) + fb_ref[...]
    v_ref[...] = _mm(n, wv_ref[...])
    k_raw = [_mm(n, wk_ref[:, c * 256:(c + 1) * 256]) for c in range(MAIN_WIDTH // 256)]
    log_f = jnp.minimum(z, 0.0) - jnp.log1p(jnp.exp(-jnp.abs(z)))
    seg = KV_CUM_SEG
    tril = (lax.broadcasted_iota(jnp.int32, (seg, seg), 0)
            >= lax.broadcasted_iota(jnp.int32, (seg, seg), 1)).astype(BF16)
    local = []
    for sgi in range(KV_TILE // seg):
        rem = log_f[sgi * seg:(sgi + 1) * seg, :]
        acc = None
        for p in range(3):
            piece = rem.astype(BF16)
            term = jnp.dot(tril, piece, preferred_element_type=F32)
            acc = term if acc is None else acc + term
            rem = rem - piece.astype(F32)
        local.append(acc)
    last = carry_ref[...]
    parts = []
    for acc in local:
        parts.append(acc + last)
        last = parts[-1][seg - 1:seg, :]
    carry_ref[...] = last
    cum = jnp.concatenate(parts, axis=0)

    lane = lax.broadcasted_iota(jnp.int32, cum.shape, 1)
    rem = jnp.where(lane < FOX_HEADS, cum * LOG2E, 0.0)
    packed = None
    for p in range(FOX_BIAS_PIECES):
        piece = rem.astype(BF16).astype(F32)
        moved = pltpu.roll(piece, p * FOX_PIECE_STRIDE, axis=1) if p else piece
        packed = moved if packed is None else packed + moved
        rem = rem - piece
    placed = _place_bias(packed.astype(BF16))
    _, slot = _bias_slot_layout()
    slot_row = slot[0:1, :]
    n_p = FOX_BIAS_PIECES
    k_bias = (placed[:, :MAIN_WIDTH]
              + ((slot_row >= n_p) & (slot_row < 2 * n_p)).astype(F32)).astype(BF16)
    qbias_ref[0] = (placed[:, MAIN_WIDTH:] - (slot_row < n_p).astype(F32)).astype(BF16)

    pair_w = 2 * FOX_HEAD_DIM
    for c in range(MAIN_WIDTH // 256):
        sl = slice(c * 256, (c + 1) * 256)
        k_n = _group_rms(k_raw[c], kg_ref[:, sl], FOX_HEAD_DIM).astype(BF16)
        for pp in range(256 // pair_w):
            lo = c * 256 + pp * pair_w
            for hh in range(2):
                dst = 2 * lo + hh * pair_w
                ka_ref[0, :, dst:dst + pair_w] = _with_bias_lanes(
                    k_n[:, pp * pair_w:(pp + 1) * pair_w], k_bias[:, lo:lo + pair_w], hh)
    vt_ref[0] = v_ref[...].T.astype(BF16)


def _kv(x, gain, w_kv, wf, f_bias, k_gain_tiled):
    row = lambda w: pl.BlockSpec((1, KV_TILE, w), lambda b, i: (b, i, 0))
    return pl.pallas_call(
        _kv_kernel,
        grid=(BATCH, SEQ // KV_TILE),
        in_specs=[row(D_MODEL), _resident((1, D_MODEL)),
                  _resident((D_MODEL, MAIN_WIDTH), (0, 0)),
                  _resident((D_MODEL, MAIN_WIDTH), (0, 1)), _resident((D_MODEL, V7X_LANES)),
                  _resident((1, V7X_LANES)), _resident((1, MAIN_WIDTH))],
        out_specs=[row(2 * MAIN_WIDTH),
                   pl.BlockSpec((1, MAIN_WIDTH, KV_TILE), lambda b, i: (b, 0, i)),
                   row(MAIN_WIDTH)],
        out_shape=[jax.ShapeDtypeStruct((BATCH, SEQ, 2 * MAIN_WIDTH), BF16),
                   jax.ShapeDtypeStruct((BATCH, MAIN_WIDTH, SEQ), BF16),
                   jax.ShapeDtypeStruct((BATCH, SEQ, MAIN_WIDTH), BF16)],
        scratch_shapes=[pltpu.VMEM((1, V7X_LANES), F32),
                        pltpu.VMEM((KV_TILE, MAIN_WIDTH), F32)],
        compiler_params=_params(("parallel", "arbitrary")),
        name="kv",
    )(x, gain, w_kv, w_kv, wf, f_bias, k_gain_tiled)


def _fox_kernel(qa_ref, ka_ref, vt_ref, gate_ref, o_ref, s_ref, m_ref, acc_ref):
    T, HD = FOX_TILE, FOX_HEAD_DIM
    W = 2 * HD
    H = T // 2

    def pieces(i, j):
        if j == i:
            return [(slice(0, H), slice(0, T)), (slice(H, T), slice(H, T))]
        return [(slice(0, T), slice(0, T))]

    def issue_scores(slot, i, j):
        for hh in range(2):
            for keys, cols in pieces(i, j):
                s_ref[slot, hh, keys, cols] = lax.dot_general(
                    ka_ref[0, j * T + keys.start:j * T + keys.stop, hh * W:(hh + 1) * W],
                    qa_ref[0, i * T + cols.start:i * T + cols.stop, hh * W:(hh + 1) * W],
                    (((1,), (1,)), ((), ())), preferred_element_type=F32)

    def consume(slot, i, j):
        for n_piece, (keys, cols) in enumerate(pieces(i, j)):
            n_k, n_q = keys.stop - keys.start, cols.stop - cols.start
            first = j == 0 and n_piece == 0
            probs, alphas = [], []
            for hh in range(2):
                s = s_ref[slot, hh, keys, cols]
                if j == i:
                    key_pos = lax.broadcasted_iota(jnp.int32, (n_k, n_q), 0) + keys.start
                    query_pos = lax.broadcasted_iota(jnp.int32, (n_k, n_q), 1) + cols.start
                    s = jnp.where(key_pos <= query_pos, s, -jnp.inf)
                m_new = jnp.max(s, axis=0, keepdims=True)
                if not first:
                    m_old = m_ref[hh, :, cols]
                    m_new = jnp.maximum(m_old, m_new)
                    alphas.append(jnp.exp2(m_old - m_new))
                probs.append(jnp.exp2(s - m_new).astype(BF16))
                m_ref[hh, :, cols] = m_new
            for hh in range(2):
                vt = jnp.concatenate(
                    [vt_ref[0, hh * HD:(hh + 1) * HD, j * T + keys.start:j * T + keys.stop],
                     jnp.ones((ONES_ROWS, n_k), BF16)], axis=0)
                pv = jnp.dot(vt, probs[hh], preferred_element_type=F32)
                acc_ref[hh, :, cols] = pv if first else alphas[hh] * acc_ref[hh, :, cols] + pv
        if j == i:
            out_t = jnp.concatenate([acc_ref[hh, 0:HD, :] / acc_ref[hh, HD:HD + 1, :]
                                     for hh in range(2)], axis=0)
            rows = slice(i * T, (i + 1) * T)
            o_ref[0, rows, :] = (out_t.T * gate_ref[0, rows, :]).astype(BF16)

    blocks = [(i, j) for i in range(SEQ // T) for j in range(i + 1)]
    issue_scores(0, *blocks[0])
    for n, (i, j) in enumerate(blocks):
        if n + 1 < len(blocks):
            issue_scores((n + 1) % 2, *blocks[n + 1])
        consume(n % 2, i, j)


def _fox(qa, ka, vt, gate):
    T = FOX_TILE
    W = 2 * FOX_HEAD_DIM
    seq_spec = lambda w: pl.BlockSpec((1, SEQ, w), lambda b, p: (b, 0, p))
    return pl.pallas_call(
        _fox_kernel,
        grid=(BATCH, FOX_HEADS // 2),
        in_specs=[seq_spec(2 * W), seq_spec(2 * W),
                  pl.BlockSpec((1, W, SEQ), lambda b, p: (b, p, 0)),
                  seq_spec(W)],
        out_specs=seq_spec(W),
        out_shape=jax.ShapeDtypeStruct((BATCH, SEQ, MAIN_WIDTH), BF16),
        scratch_shapes=[pltpu.VMEM((2, 2, T, T), F32),
                        pltpu.VMEM((2, 1, T), F32),
                        pltpu.VMEM((2, FOX_HEAD_DIM + ONES_ROWS, T), F32)],
        compiler_params=_params(("parallel", "parallel")),
        name="fox",
    )(qa, ka, vt, gate)


def _memattn_kernel(qm_ref, mem_ref, mgain_ref, wkv_ref, qg_ref, kg_ref, o_ref, kv_ref):
    W, HD = MEM_WIDTH, MEM_HEAD_DIM
    batch = range(MEM_BATCH)
    heads = range(MEM_HEADS)
    ones_rows = jnp.ones((ONES_ROWS, N_MEM), BF16)
    scale = HD ** -0.5 * LOG2E

    for bi in batch:
        mem_n = _rms(mem_ref[bi], mgain_ref[...]).astype(BF16)
        kv_ref[bi] = _mm(mem_n, wkv_ref[...])
    keys, values_t = [], []
    for bi in batch:
        km = _group_rms(kv_ref[bi, :, :W], kg_ref[...], HD).astype(BF16)
        head_of_lane = lax.broadcasted_iota(jnp.int32, km.shape, 1) // HD
        vm_t = kv_ref[bi, :, W:].T.astype(BF16)
        keys.append([jnp.where(head_of_lane == h, km, jnp.zeros_like(km)) for h in heads])
        values_t.append([jnp.concatenate([vm_t[h * HD:(h + 1) * HD, :], ones_rows], axis=0)
                         for h in heads])
    queries = [(_group_rms(qm_ref[bi], qg_ref[...], HD) * scale).astype(BF16) for bi in batch]
    scores = [[lax.dot_general(keys[bi][h], queries[bi], (((1,), (1,)), ((), ())),
                               preferred_element_type=F32) for h in heads] for bi in batch]
    for bi in batch:
        outs = []
        for h in heads:
            s = scores[bi][h]
            p = jnp.exp2(s - jnp.max(s, axis=0, keepdims=True)).astype(BF16)
            r = jnp.dot(values_t[bi][h], p, preferred_element_type=F32)
            outs.append(r[0:HD, :] / r[HD:HD + 1, :])
        o_ref[bi] = jnp.concatenate(outs, axis=0).T.astype(BF16)


def _memattn(layer, qm, mem, mem_gains, w_mem_kv, q_gains_tiled, k_gains_tiled):
    qspec = pl.BlockSpec((MEM_BATCH, SEQ, MEM_WIDTH), lambda b: (b, 0, 0))
    return pl.pallas_call(
        _memattn_kernel,
        grid=(BATCH // MEM_BATCH,),
        in_specs=[qspec,
                  pl.BlockSpec((MEM_BATCH, N_MEM, D_MODEL), lambda b: (b, 0, 0)),
                  _layer((1, D_MODEL), layer), _layer((D_MODEL, 2 * MEM_WIDTH), layer),
                  _layer((1, MEM_WIDTH), layer), _layer((1, MEM_WIDTH), layer)],
        out_specs=qspec,
        out_shape=jax.ShapeDtypeStruct((BATCH, SEQ, MEM_WIDTH), BF16),
        scratch_shapes=[pltpu.VMEM((MEM_BATCH, N_MEM, 2 * MEM_WIDTH), F32)],
        compiler_params=_params(("parallel",)),
        name="memattn",
    )(qm, mem, mem_gains, w_mem_kv, q_gains_tiled, k_gains_tiled)


def kernel(x, mem, ffn1_norm, ffn1_w_gate, ffn1_w_up, ffn1_w_down, mix_norm, mem_norm,
           w_mem_kv, mem_q_gain, mem_k_gain, w_in_a, hgrn_lb_logits, hgrn_o_gain,
           w_in_b, fox_q_gain, kv_norm, w_kv, fox_f_bias, fox_k_gain, w_out,
           ffn2_norm, ffn2_w_gate, ffn2_w_up, ffn2_w_down):
    row = lambda t: t.reshape(1, -1).astype(F32)
    rows = lambda t, reps=1: jnp.tile(t.astype(F32), (1, reps)).reshape(t.shape[0], 1, -1)
    f32 = lambda t: t.astype(F32)
    x = x.reshape(TOKENS, D_MODEL).astype(F32)
    mem = mem.astype(F32)
    lb_logits = hgrn_lb_logits.astype(F32)
    ffn1 = (rows(ffn1_norm), f32(ffn1_w_gate), f32(ffn1_w_up), f32(ffn1_w_down))
    ffn2 = (rows(ffn2_norm), f32(ffn2_w_gate), f32(ffn2_w_up), f32(ffn2_w_down))
    mix_gains = rows(mix_norm)
    o_gains, fox_q_gains = rows(hgrn_o_gain), rows(fox_q_gain, FOX_HEADS)
    mem_args = (rows(mem_norm), f32(w_mem_kv), rows(mem_q_gain, MEM_HEADS),
                rows(mem_k_gain, MEM_HEADS))

    for l in range(DEPTH):
        x = _ffn(l, x, *ffn1)
        if l < N_A_LAYERS:
            qs, lf, iv, sg, qm = _inproj_a(l, x, mix_gains, f32(w_in_a), lb_logits)
            to3 = lambda t: t.reshape(BATCH, SEQ, MAIN_WIDTH)
            main = _hgrn(l, to3(qs), to3(lf), to3(iv), to3(sg), o_gains)
        else:
            qa, gate, qm = _inproj_b(l, x, mix_gains, f32(w_in_b), fox_q_gains,
                                     q_bias.reshape(TOKENS, MAIN_WIDTH))
            main = _fox(qa.reshape(BATCH, SEQ, 2 * MAIN_WIDTH), k_aug, v_t,
                        gate.reshape(BATCH, SEQ, MAIN_WIDTH))
        memo = _memattn(l, qm.reshape(BATCH, SEQ, MEM_WIDTH), mem, *mem_args)
        x = _mix_ffn(l, x, main.reshape(TOKENS, MAIN_WIDTH), memo.reshape(TOKENS, MEM_WIDTH),
                     f32(w_out), *ffn2)
        if l == N_A_LAYERS - 1:
            pad = V7X_LANES - FOX_HEADS
            wf = jnp.pad(f32(w_kv[:, 2 * MAIN_WIDTH:]), ((0, 0), (0, pad)))
            fb = jnp.pad(fox_f_bias.astype(F32), (0, pad)).reshape(1, -1)
            k_aug, v_t, q_bias = _kv(x.reshape(BATCH, SEQ, D_MODEL), row(kv_norm),
                                     f32(w_kv), wf, fb,
                                     row(jnp.tile(fox_k_gain, FOX_HEADS)))
    return x.reshape(BATCH, SEQ, D_MODEL)
```

```python
import functools

import jax
import jax.numpy as jnp
from jax import lax
from jax.experimental import pallas as pl
from jax.experimental.pallas import tpu as pltpu

F32 = jnp.float32
BF16 = jnp.bfloat16

D_MODEL = 1024
BATCH = 8
SEQ = 2048
DEPTH = 4
N_MEM = 256
N_A_LAYERS = DEPTH // 2
MAIN_WIDTH = 768
MEM_WIDTH = 256
HG_HEAD_DIM = 128
HG_HEADS = MAIN_WIDTH // HG_HEAD_DIM
FOX_HEAD_DIM = 64
FOX_HEADS = MAIN_WIDTH // FOX_HEAD_DIM
MEM_HEADS = 4
MEM_HEAD_DIM = MEM_WIDTH // MEM_HEADS
D_FF = 2816
EPS = 1e-6
TOKENS = BATCH * SEQ

V7X_LANES = 128
V7X_VMEM_BYTES = 64 * 1024 * 1024

ROW_TILE = 512
FF_CHUNK = 256
FFN_STAGE_SLOTS = 2
HG_CHUNK = 64
HG_SUB = 16
HG_SAFE_DECAY = 60.0
HG_GROUP = 4
HG_HEADS_PER_STEP = 2
FOX_TILE = 512
ONES_ROWS = 16
MEM_BATCH = 2
KV_TILE = 512
KV_CUM_SEG = 256
VMEM_LIMIT = 56 * 1024 * 1024


def _rms(x, gain):
    ms = jnp.mean(x * x, axis=-1, keepdims=True)
    return x * lax.rsqrt(ms + EPS) * gain


def _mm(a, w):
    return jnp.dot(a, w.astype(BF16), preferred_element_type=F32)


def _split_dot(x, rhs01, parts):
    acc = None
    rem = x
    for p in range(parts):
        piece = rem.astype(BF16)
        term = jnp.dot(piece, rhs01, preferred_element_type=F32)
        acc = term if acc is None else acc + term
        if p + 1 < parts:
            rem = rem - piece.astype(F32)
    return acc


def _group_ones(width, group):
    r = lax.broadcasted_iota(jnp.int32, (width, width), 0) // group
    c = lax.broadcasted_iota(jnp.int32, (width, width), 1) // group
    return (r == c).astype(BF16)


def _group_rms(x, gain_tiled, group):
    width = x.shape[-1]
    ss = _split_dot(x * x, _group_ones(width, group), 2)
    return x * lax.rsqrt(ss * (1.0 / group) + EPS) * gain_tiled


def _resident(shape, index=None):
    index = (0,) * len(shape) if index is None else index
    return pl.BlockSpec(shape, lambda *_: index, pipeline_mode=pl.Buffered(1))


def _layer(shape, layer, first=0):
    index = (layer, first) + (0,) * (len(shape) - 1)
    return pl.BlockSpec((None,) + tuple(shape), lambda *_: index,
                        pipeline_mode=pl.Buffered(1))


def _params(sem):
    return pltpu.CompilerParams(dimension_semantics=sem, vmem_limit_bytes=VMEM_LIMIT)


def _swiglu_weight_stream(layer, hbm, resident, stage_gu_ref, stage_d_ref, sem):
    wg_hbm, wu_hbm, wd_hbm = hbm
    wg_ref, wu_ref, wd_ref = resident

    def copies(c):
        slot = c % FFN_STAGE_SLOTS
        span = pl.ds(c * FF_CHUNK, FF_CHUNK)
        return (pltpu.make_async_copy(wg_hbm.at[layer, :, span], stage_gu_ref.at[0, slot],
                                      sem.at[0, slot]),
                pltpu.make_async_copy(wu_hbm.at[layer, :, span], stage_gu_ref.at[1, slot],
                                      sem.at[1, slot]),
                pltpu.make_async_copy(wd_hbm.at[layer, span, :], stage_d_ref.at[slot],
                                      sem.at[2, slot]))

    def start(c):
        for copy in copies(c):
            copy.start()

    def fetch(c):
        slot = c % FFN_STAGE_SLOTS
        sl = slice(c * FF_CHUNK, (c + 1) * FF_CHUNK)
        for copy in copies(c):
            copy.wait()
        wg_ref[:, sl] = stage_gu_ref[0, slot].astype(BF16)
        wu_ref[:, sl] = stage_gu_ref[1, slot].astype(BF16)
        wd_ref[sl, :] = stage_d_ref[slot].astype(BF16)
        if c + FFN_STAGE_SLOTS < D_FF // FF_CHUNK:
            start(c + FFN_STAGE_SLOTS)

    return start, fetch


def _swiglu_rows(x, gain_ref, resident, h_ref, fetch):
    wg_ref, wu_ref, wd_ref = resident
    n = _rms(x, gain_ref[...]).astype(BF16)
    for c in range(D_FF // FF_CHUNK):
        if fetch is not None:
            fetch(c)
        sl = slice(c * FF_CHUNK, (c + 1) * FF_CHUNK)
        g = jnp.dot(n, wg_ref[:, sl], preferred_element_type=F32)
        u = jnp.dot(n, wu_ref[:, sl], preferred_element_type=F32)
        h_ref[:, sl] = (g * jax.nn.sigmoid(g) * u).astype(BF16)
    y = jnp.dot(h_ref[...], wd_ref[...], preferred_element_type=F32)
    return x + 0.5 * y


def _swiglu_steps(layer, read_x, gain_ref, hbm, o_ref, scratch):
    h_ref, wg_ref, wu_ref, wd_ref, stage_gu_ref, stage_d_ref, sem = scratch
    resident = (wg_ref, wu_ref, wd_ref)
    start, fetch = _swiglu_weight_stream(layer, hbm, resident, stage_gu_ref, stage_d_ref, sem)

    @pl.when(pl.program_id(0) == 0)
    def _():
        for c in range(FFN_STAGE_SLOTS):
            start(c)
        o_ref[...] = _swiglu_rows(read_x(), gain_ref, resident, h_ref, fetch)

    @pl.when(pl.program_id(0) > 0)
    def _():
        o_ref[...] = _swiglu_rows(read_x(), gain_ref, resident, h_ref, None)


def _ffn_kernel(layer, x_ref, gain_ref, wg_hbm, wu_hbm, wd_hbm, o_ref, *scratch):
    _swiglu_steps(layer, lambda: x_ref[...], gain_ref, (wg_hbm, wu_hbm, wd_hbm), o_ref, scratch)


def _mix_ffn_kernel(layer, x_ref, main_ref, memo_ref, wo_main_ref, wo_mem_ref,
                    gain_ref, wg_hbm, wu_hbm, wd_hbm, o_ref, *scratch):
    def mixed_x():
        return (x_ref[...]
                + _mm(main_ref[...], wo_main_ref[...])
                + _mm(memo_ref[...], wo_mem_ref[...]))
    _swiglu_steps(layer, mixed_x, gain_ref, (wg_hbm, wu_hbm, wd_hbm), o_ref, scratch)


def _row_spec(width):
    return pl.BlockSpec((ROW_TILE, width), lambda i: (i, 0))


_IN_HBM = pl.BlockSpec(memory_space=pl.ANY)
_SWIGLU_SCRATCH = [
    pltpu.VMEM((ROW_TILE, D_FF), BF16),
    pltpu.VMEM((D_MODEL, D_FF), BF16),
    pltpu.VMEM((D_MODEL, D_FF), BF16),
    pltpu.VMEM((D_FF, D_MODEL), BF16),
    pltpu.VMEM((2, FFN_STAGE_SLOTS, D_MODEL, FF_CHUNK), F32),
    pltpu.VMEM((FFN_STAGE_SLOTS, FF_CHUNK, D_MODEL), F32),
    pltpu.SemaphoreType.DMA((3, FFN_STAGE_SLOTS)),
]


def _ffn(layer, x, gains, wg, wu, wd):
    return pl.pallas_call(
        functools.partial(_ffn_kernel, layer),
        grid=(TOKENS // ROW_TILE,),
        in_specs=[_row_spec(D_MODEL), _layer((1, D_MODEL), layer), _IN_HBM, _IN_HBM, _IN_HBM],
        out_specs=_row_spec(D_MODEL),
        out_shape=jax.ShapeDtypeStruct((TOKENS, D_MODEL), F32),
        scratch_shapes=_SWIGLU_SCRATCH,
        compiler_params=_params(("arbitrary",)),
        name="ffn",
    )(x, gains, wg, wu, wd)


def _mix_ffn(layer, x, main, memo, w_out, gains, wg, wu, wd):
    assert MAIN_WIDTH % MEM_WIDTH == 0
    return pl.pallas_call(
        functools.partial(_mix_ffn_kernel, layer),
        grid=(TOKENS // ROW_TILE,),
        in_specs=[_row_spec(D_MODEL), _row_spec(MAIN_WIDTH), _row_spec(MEM_WIDTH),
                  _layer((MAIN_WIDTH, D_MODEL), layer),
                  _layer((MEM_WIDTH, D_MODEL), layer, MAIN_WIDTH // MEM_WIDTH),
                  _layer((1, D_MODEL), layer), _IN_HBM, _IN_HBM, _IN_HBM],
        out_specs=_row_spec(D_MODEL),
        out_shape=jax.ShapeDtypeStruct((TOKENS, D_MODEL), F32),
        scratch_shapes=_SWIGLU_SCRATCH,
        compiler_params=_params(("arbitrary",)),
        name="mix_ffn",
    )(x, main, memo, w_out, w_out, gains, wg, wu, wd)


def _inproj_a_kernel(layer, x_ref, gain_ref, w_ref, lbl_ref,
                     q_ref, lf_ref, i_ref, g_ref, qm_ref):
    n = _rms(x_ref[...], gain_ref[...]).astype(BF16)
    rows = [lbl_ref[r:r + 1, :] for r in range(N_A_LAYERS)]
    top = functools.reduce(jnp.maximum, rows)
    exps = [jnp.exp(r - top) for r in rows]
    total = functools.reduce(jnp.add, exps)
    lb = jnp.zeros_like(top)
    for r in range(1, layer + 1):
        lb = lb + exps[r] / total

    w = MAIN_WIDTH
    q_raw = _mm(n, w_ref[:, 0:w])
    q_ref[...] = q_raw * jax.nn.sigmoid(q_raw)
    f_raw = _mm(n, w_ref[:, w:2 * w])
    lf_ref[...] = jnp.log(lb + (1.0 - lb) * jax.nn.sigmoid(f_raw))
    i_ref[...] = _mm(n, w_ref[:, 2 * w:3 * w]).astype(BF16)
    g_raw = _mm(n, w_ref[:, 3 * w:4 * w])
    g_ref[...] = g_raw * jax.nn.sigmoid(g_raw)
    qm_ref[...] = _mm(n, w_ref[:, 4 * w:])


def _inproj_a(layer, x, gains, w_in, lb_logits):
    a_in = 4 * MAIN_WIDTH + MEM_WIDTH
    return pl.pallas_call(
        functools.partial(_inproj_a_kernel, layer),
        grid=(TOKENS // ROW_TILE,),
        in_specs=[_row_spec(D_MODEL), _layer((1, D_MODEL), layer),
                  _layer((D_MODEL, a_in), layer), _resident((N_A_LAYERS, MAIN_WIDTH))],
        out_specs=[_row_spec(MAIN_WIDTH), _row_spec(MAIN_WIDTH), _row_spec(MAIN_WIDTH),
                   _row_spec(MAIN_WIDTH), _row_spec(MEM_WIDTH)],
        out_shape=[jax.ShapeDtypeStruct((TOKENS, MAIN_WIDTH), F32),
                   jax.ShapeDtypeStruct((TOKENS, MAIN_WIDTH), F32),
                   jax.ShapeDtypeStruct((TOKENS, MAIN_WIDTH), BF16),
                   jax.ShapeDtypeStruct((TOKENS, MAIN_WIDTH), F32),
                   jax.ShapeDtypeStruct((TOKENS, MEM_WIDTH), F32)],
        compiler_params=_params(("parallel",)),
        name="inproj_a",
    )(x, gains, w_in, lb_logits)


def _inproj_b_kernel(x_ref, gain_ref, w_ref, qg_ref, qbias_ref, qa_ref, gate_ref, qm_ref):
    n = _rms(x_ref[...], gain_ref[...]).astype(BF16)
    w = MAIN_WIDTH
    scale = FOX_HEAD_DIM ** -0.5 * LOG2E
    pair_w = 2 * FOX_HEAD_DIM
    q_raw = [_mm(n, w_ref[:, c * 256:(c + 1) * 256]) for c in range(w // 256)]
    gate = _mm(n, w_ref[:, w:2 * w])
    qm_ref[...] = _mm(n, w_ref[:, 2 * w:])
    for c in range(w // 256):
        sl = slice(c * 256, (c + 1) * 256)
        q_n = (_group_rms(q_raw[c], qg_ref[:, sl], FOX_HEAD_DIM) * scale).astype(BF16)
        for pp in range(256 // pair_w):
            lo = c * 256 + pp * pair_w
            for hh in range(2):
                dst = 2 * lo + hh * pair_w
                qa_ref[:, dst:dst + pair_w] = _with_bias_lanes(
                    q_n[:, pp * pair_w:(pp + 1) * pair_w], qbias_ref[:, lo:lo + pair_w], hh)
    gate_ref[...] = jax.nn.sigmoid(gate)


def _inproj_b(layer, x, gains, w_in, q_gain_tiled, q_bias):
    b_in = 2 * MAIN_WIDTH + MEM_WIDTH
    j = layer - N_A_LAYERS
    return pl.pallas_call(
        _inproj_b_kernel,
        grid=(TOKENS // ROW_TILE,),
        in_specs=[_row_spec(D_MODEL), _layer((1, D_MODEL), layer),
                  _layer((D_MODEL, b_in), j), _layer((1, MAIN_WIDTH), j),
                  _row_spec(MAIN_WIDTH)],
        out_specs=[_row_spec(2 * MAIN_WIDTH), _row_spec(MAIN_WIDTH), _row_spec(MEM_WIDTH)],
        out_shape=[jax.ShapeDtypeStruct((TOKENS, 2 * MAIN_WIDTH), BF16),
                   jax.ShapeDtypeStruct((TOKENS, MAIN_WIDTH), F32),
                   jax.ShapeDtypeStruct((TOKENS, MEM_WIDTH), F32)],
        compiler_params=_params(("parallel",)),
        name="inproj_b",
    )(x, gains, w_in, q_gain_tiled, q_bias)


def _hgrn_kernel(q_ref, lf_ref, v_ref, g_ref, gain_ref, o_ref,
                 cpad_ref, kpad_ref, vpad_ref):
    C, SB, HD = HG_CHUNK, HG_SUB, HG_HEAD_DIM
    n_sub = C // SB
    r_i = lax.broadcasted_iota(jnp.int32, (C, C), 0)
    c_i = lax.broadcasted_iota(jnp.int32, (C, C), 1)
    tril = (r_i >= c_i).astype(BF16)
    gain = gain_ref[...]

    def intra_exact(q, k, v16, c):
        outs = [jnp.zeros((SB, HD), F32)]
        for b in range(1, n_sub):
            lo = b * SB
            ref = c[lo - 1:lo, :]
            qb = (q[lo:lo + SB, :] * jnp.exp(c[lo:lo + SB, :] - ref)).astype(BF16)
            kb = (k[0:lo, :] * jnp.exp(ref - c[0:lo, :])).astype(BF16)
            p_b = lax.dot_general(qb, kb, (((1,), (1,)), ((), ())),
                                  preferred_element_type=F32)
            outs.append(jnp.dot(p_b.astype(BF16), v16[0:lo, :], preferred_element_type=F32))
        o = jnp.concatenate(outs, axis=0)
        ones = jnp.ones((HD, HD), BF16)
        row_in_block = lax.broadcasted_iota(jnp.int32, (C, HD), 0) % SB
        cpad_ref[SB:SB + C, :] = c
        kpad_ref[SB:SB + C, :] = k
        vpad_ref[SB:SB + C, :] = v16.astype(F32)
        for d in range(SB):
            cs = cpad_ref[SB - d:SB - d + C, :]
            ks = kpad_ref[SB - d:SB - d + C, :]
            vs = vpad_ref[SB - d:SB - d + C, :]
            wgt = jnp.where(row_in_block >= d, q * ks * jnp.exp(c - cs), 0.0)
            s = jnp.dot(wgt.astype(BF16), ones, preferred_element_type=F32)
            o = o + s * vs
        return o

    def chunk(n, st, intra, lanes):
        r0 = pl.multiple_of(n * C, C)
        lf = lf_ref[0, pl.ds(r0, C), lanes]
        q = q_ref[0, pl.ds(r0, C), lanes]
        v16 = v_ref[0, pl.ds(r0, C), lanes]
        k = 1.0 - jnp.exp(lf)
        c = None
        rem = lf
        for p in range(3):
            piece = rem.astype(BF16)
            term = jnp.dot(tril, piece, preferred_element_type=F32)
            c = term if c is None else c + term
            rem = rem - piece.astype(F32)
        c_end = c[C - 1:C, :]

        qe = (q * jnp.exp(c)).astype(BF16)
        o = lax.dot_general(qe, st.astype(BF16), (((1,), (1,)), ((), ())),
                            preferred_element_type=F32)
        kd = (k * jnp.exp(c_end - c)).astype(BF16)
        upd = lax.dot_general(v16, kd, (((0,), (0,)), ((), ())),
                              preferred_element_type=F32)
        o = o + intra(q, k, v16, c)
        o_ref[0, pl.ds(r0, C), lanes] = (
            _rms(o, gain) * g_ref[0, pl.ds(r0, C), lanes]).astype(BF16)
        return st * jnp.exp(c_end) + upd

    G = HG_GROUP
    R = G * C
    gr = lax.broadcasted_iota(jnp.int32, (R, R), 0)
    gc = lax.broadcasted_iota(jnp.int32, (R, R), 1)
    tril_group = ((gr // C == gc // C) & (gr >= gc)).astype(BF16)
    causal = r_i >= c_i

    def stage_cumsum(i):
        lf = lf_ref[0, i * R:(i + 1) * R, :]
        c = None
        rem = lf
        for p in range(3):
            piece = rem.astype(BF16)
            term = jnp.dot(tril_group, piece, preferred_element_type=F32)
            c = term if c is None else c + term
            rem = rem - piece.astype(F32)
        return lf, c

    def stage_scores(i, lanes, lf, c):
        lf, c = lf[:, lanes], c[:, lanes]
        q = q_ref[0, i * R:(i + 1) * R, lanes]
        v16 = v_ref[0, i * R:(i + 1) * R, lanes]
        k = 1.0 - jnp.exp(lf)
        lhs, rhs, c_ends = [], [], []
        for g in range(G):
            cg = c[g * C:(g + 1) * C, :]
            qg = q[g * C:(g + 1) * C, :]
            kg = k[g * C:(g + 1) * C, :]
            refs = [jnp.zeros((1, HD), F32)] + [cg[b * SB - 1:b * SB, :] for b in range(1, n_sub)]
            own_ref = jnp.concatenate([jnp.broadcast_to(r, (SB, HD)) for r in refs], axis=0)
            k_own = (kg * jnp.exp(own_ref - cg)).astype(BF16)
            lhs_cols, rhs_cols = [], []
            for b in range(n_sub):
                lo = b * SB
                q_b = (qg[lo:, :] * jnp.exp(cg[lo:, :] - refs[b])).astype(BF16)
                k_b = k_own[lo:lo + SB, :]
                above = [jnp.zeros((lo, HD), BF16)] if lo else []
                below = [jnp.zeros((C - lo - SB, HD), BF16)] if lo + SB < C else []
                lhs_cols.append(jnp.concatenate(above + [q_b], axis=0))
                rhs_cols.append(jnp.concatenate(above + [k_b] + below, axis=0))
            lhs.append(jnp.concatenate(lhs_cols, axis=1))
            rhs.append(jnp.concatenate(rhs_cols, axis=1))
            c_ends.append(cg[C - 1:C, :])
        c_end_rows = jnp.concatenate([jnp.broadcast_to(e, (C, HD)) for e in c_ends], axis=0)
        qe = (q * jnp.exp(c)).astype(BF16)
        kd = (k * jnp.exp(c_end_rows - c)).astype(BF16)

        scores = [lax.dot_general(lhs[g], rhs[g], (((1,), (1,)), ((), ())),
                                  preferred_element_type=F32) for g in range(G)]
        upds = [lax.dot_general(v16[g * C:(g + 1) * C, :], kd[g * C:(g + 1) * C, :],
                                (((0,), (0,)), ((), ())), preferred_element_type=F32)
                for g in range(G)]
        return scores, upds, qe, c_ends

    def stage_output(i, lanes, prepared, st):
        scores, upds, qe, c_ends = prepared
        v16 = v_ref[0, i * R:(i + 1) * R, lanes]
        intra = [jnp.dot(jnp.where(causal, scores[g], 0.0).astype(BF16),
                         v16[g * C:(g + 1) * C, :], preferred_element_type=F32)
                 for g in range(G)]
        outs = []
        for g in range(G):
            inter = lax.dot_general(qe[g * C:(g + 1) * C, :], st.astype(BF16),
                                    (((1,), (1,)), ((), ())), preferred_element_type=F32)
            outs.append(inter + intra[g])
            st = st * jnp.exp(c_ends[g]) + upds[g]
        o = jnp.concatenate(outs, axis=0)
        rows = slice(i * R, (i + 1) * R)
        o_ref[0, rows, lanes] = (_rms(o, gain) * g_ref[0, rows, lanes]).astype(BF16)
        return st

    head_lanes = [slice(hh * HD, (hh + 1) * HD) for hh in range(HG_HEADS_PER_STEP)]

    def run_factored():
        n_groups = SEQ // R
        first = stage_cumsum(0)
        prepared = [stage_scores(0, lanes, *first) for lanes in head_lanes]
        states = [jnp.zeros((HD, HD), F32) for _ in head_lanes]
        for i in range(n_groups):
            if i + 1 < n_groups:
                ahead = stage_cumsum(i + 1)
            states = [stage_output(i, lanes, prepared[hh], states[hh])
                      for hh, lanes in enumerate(head_lanes)]
            if i + 1 < n_groups:
                prepared = [stage_scores(i + 1, lanes, *ahead) for lanes in head_lanes]

    def run_exact():
        for lanes in head_lanes:
            def body(n, st, lanes=lanes):
                return chunk(n, st, intra_exact, lanes)
            lax.fori_loop(0, SEQ // C, body, jnp.zeros((HD, HD), F32))

    blk = lax.broadcasted_iota(jnp.int32, (SEQ // SB, SEQ), 0)
    tok = lax.broadcasted_iota(jnp.int32, (SEQ // SB, SEQ), 1) // SB
    block_decay = jnp.dot((blk == tok).astype(BF16), lf_ref[0].astype(BF16),
                          preferred_element_type=F32)
    mild = jnp.min(block_decay) >= -HG_SAFE_DECAY

    @pl.when(mild)
    def _():
        run_factored()

    @pl.when(jnp.logical_not(mild))
    def _():
        zpad = jnp.zeros((SB, HD), F32)
        cpad_ref[0:SB, :] = zpad
        kpad_ref[0:SB, :] = zpad
        vpad_ref[0:SB, :] = zpad
        run_exact()


def _hgrn(layer, q, lf, v, g, o_gains):
    spec = pl.BlockSpec((1, SEQ, HG_HEADS_PER_STEP * HG_HEAD_DIM), lambda b, h: (b, 0, h))
    return pl.pallas_call(
        _hgrn_kernel,
        grid=(BATCH, HG_HEADS // HG_HEADS_PER_STEP),
        in_specs=[spec, spec, spec, spec, _layer((1, HG_HEAD_DIM), layer)],
        out_specs=spec,
        out_shape=jax.ShapeDtypeStruct((BATCH, SEQ, MAIN_WIDTH), BF16),
        scratch_shapes=[pltpu.VMEM((HG_SUB + HG_CHUNK, HG_HEAD_DIM), F32),
                        pltpu.VMEM((HG_SUB + HG_CHUNK, HG_HEAD_DIM), F32),
                        pltpu.VMEM((HG_SUB + HG_CHUNK, HG_HEAD_DIM), F32)],
        compiler_params=_params(("parallel", "parallel")),
        name="hgrn",
    )(q, lf, v, g, o_gains)


FOX_BIAS_PIECES = 3
FOX_PIECE_STRIDE = 16
LOG2E = 1.4426950408889634


def _bias_slot_layout():
    lane = lax.broadcasted_iota(jnp.int32, (V7X_LANES, MAIN_WIDTH), 1)
    pair = lane // (2 * FOX_HEAD_DIM)
    half = (lane // FOX_HEAD_DIM) % 2
    head_here = 2 * pair + (1 - half)
    slot = lane % FOX_HEAD_DIM
    return head_here, slot


def _place_bias(packed):
    head_here, slot = _bias_slot_layout()
    src = lax.broadcasted_iota(jnp.int32, (V7X_LANES, MAIN_WIDTH), 0)
    src_piece, src_head = src // FOX_PIECE_STRIDE, src % FOX_PIECE_STRIDE
    hit = (src_piece < FOX_BIAS_PIECES) & (src_head == head_here)
    place = jnp.concatenate([(hit & (slot == src_piece)).astype(BF16),
                             (hit & (slot == src_piece + FOX_BIAS_PIECES)).astype(BF16)],
                            axis=1)
    return jnp.dot(packed, place, preferred_element_type=F32)


def _with_bias_lanes(pair_vals, pair_bias, head_in_pair):
    own = (lax.broadcasted_iota(jnp.int32, pair_vals.shape, 1) // FOX_HEAD_DIM) == head_in_pair
    return jnp.where(own, pair_vals, pair_bias)


def _kv_kernel(x_ref, gain_ref, wk_ref, wv_ref, wf_ref, fb_ref, kg_ref,
               ka_ref, vt_ref, qbias_ref, carry_ref, v_ref):
    @pl.when(pl.program_id(1) == 0)
    def _():
        carry_ref[...] = jnp.zeros_like(carry_ref)

    n = _rms(x_ref[0], gain_ref[...]).astype(BF16)

    z = _mm(n, wf_ref[...]) + fb_ref[...]
    v_ref[...] = _mm(n, wv_ref[...])
    k_raw = [_mm(n, wk_ref[:, c * 256:(c + 1) * 256]) for c in range(MAIN_WIDTH // 256)]
    log_f = jnp.minimum(z, 0.0) - jnp.log1p(jnp.exp(-jnp.abs(z)))
    seg = KV_CUM_SEG
    tril = (lax.broadcasted_iota(jnp.int32, (seg, seg), 0)
            >= lax.broadcasted_iota(jnp.int32, (seg, seg), 1)).astype(BF16)
    local = []
    for sgi in range(KV_TILE // seg):
        rem = log_f[sgi * seg:(sgi + 1) * seg, :]
        acc = None
        for p in range(3):
            piece = rem.astype(BF16)
            term = jnp.dot(tril, piece, preferred_element_type=F32)
            acc = term if acc is None else acc + term
            rem = rem - piece.astype(F32)
        local.append(acc)
    last = carry_ref[...]
    parts = []
    for acc in local:
        parts.append(acc + last)
        last = parts[-1][seg - 1:seg, :]
    carry_ref[...] = last
    cum = jnp.concatenate(parts, axis=0)

    lane = lax.broadcasted_iota(jnp.int32, cum.shape, 1)
    rem = jnp.where(lane < FOX_HEADS, cum * LOG2E, 0.0)
    packed = None
    for p in range(FOX_BIAS_PIECES):
        piece = rem.astype(BF16).astype(F32)
        moved = pltpu.roll(piece, p * FOX_PIECE_STRIDE, axis=1) if p else piece
        packed = moved if packed is None else packed + moved
        rem = rem - piece
    placed = _place_bias(packed.astype(BF16))
    _, slot = _bias_slot_layout()
    slot_row = slot[0:1, :]
    n_p = FOX_BIAS_PIECES
    k_bias = (placed[:, :MAIN_WIDTH]
              + ((slot_row >= n_p) & (slot_row < 2 * n_p)).astype(F32)).astype(BF16)
    qbias_ref[0] = (placed[:, MAIN_WIDTH:] - (slot_row < n_p).astype(F32)).astype(BF16)

    pair_w = 2 * FOX_HEAD_DIM
    for c in range(MAIN_WIDTH // 256):
        sl = slice(c * 256, (c + 1) * 256)
        k_n = _group_rms(k_raw[c], kg_ref[:, sl], FOX_HEAD_DIM).astype(BF16)
        for pp in range(256 // pair_w):
            lo = c * 256 + pp * pair_w
            for hh in range(2):
                dst = 2 * lo + hh * pair_w
                ka_ref[0, :, dst:dst + pair_w] = _with_bias_lanes(
                    k_n[:, pp * pair_w:(pp + 1) * pair_w], k_bias[:, lo:lo + pair_w], hh)
    vt_ref[0] = v_ref[...].T.astype(BF16)


def _kv(x, gain, w_kv, wf, f_bias, k_gain_tiled):
    row = lambda w: pl.BlockSpec((1, KV_TILE, w), lambda b, i: (b, i, 0))
    return pl.pallas_call(
        _kv_kernel,
        grid=(BATCH, SEQ // KV_TILE),
        in_specs=[row(D_MODEL), _resident((1, D_MODEL)),
                  _resident((D_MODEL, MAIN_WIDTH), (0, 0)),
                  _resident((D_MODEL, MAIN_WIDTH), (0, 1)), _resident((D_MODEL, V7X_LANES)),
                  _resident((1, V7X_LANES)), _resident((1, MAIN_WIDTH))],
        out_specs=[row(2 * MAIN_WIDTH),
                   pl.BlockSpec((1, MAIN_WIDTH, KV_TILE), lambda b, i: (b, 0, i)),
                   row(MAIN_WIDTH)],
        out_shape=[jax.ShapeDtypeStruct((BATCH, SEQ, 2 * MAIN_WIDTH), BF16),
                   jax.ShapeDtypeStruct((BATCH, MAIN_WIDTH, SEQ), BF16),
                   jax.ShapeDtypeStruct((BATCH, SEQ, MAIN_WIDTH), BF16)],
        scratch_shapes=[pltpu.VMEM((1, V7X_LANES), F32),
                        pltpu.VMEM((KV_TILE, MAIN_WIDTH), F32)],
        compiler_params=_params(("parallel", "arbitrary")),
        name="kv",
    )(x, gain, w_kv, w_kv, wf, f_bias, k_gain_tiled)


def _fox_kernel(qa_ref, ka_ref, vt_ref, gate_ref, o_ref, s_ref, m_ref, acc_ref):
    T, HD = FOX_TILE, FOX_HEAD_DIM
    W = 2 * HD
    H = T // 2

    def pieces(i, j):
        if j == i:
            return [(slice(0, H), slice(0, T)), (slice(H, T), slice(H, T))]
        return [(slice(0, T), slice(0, T))]

    def issue_scores(slot, i, j):
        for hh in range(2):
            for keys, cols in pieces(i, j):
                s_ref[slot, hh, keys, cols] = lax.dot_general(
                    ka_ref[0, j * T + keys.start:j * T + keys.stop, hh * W:(hh + 1) * W],
                    qa_ref[0, i * T + cols.start:i * T + cols.stop, hh * W:(hh + 1) * W],
                    (((1,), (1,)), ((), ())), preferred_element_type=F32)

    def consume(slot, i, j):
        for n_piece, (keys, cols) in enumerate(pieces(i, j)):
            n_k, n_q = keys.stop - keys.start, cols.stop - cols.start
            first = j == 0 and n_piece == 0
            probs, alphas = [], []
            for hh in range(2):
                s = s_ref[slot, hh, keys, cols]
                if j == i:
                    key_pos = lax.broadcasted_iota(jnp.int32, (n_k, n_q), 0) + keys.start
                    query_pos = lax.broadcasted_iota(jnp.int32, (n_k, n_q), 1) + cols.start
                    s = jnp.where(key_pos <= query_pos, s, -jnp.inf)
                m_new = jnp.max(s, axis=0, keepdims=True)
                if not first:
                    m_old = m_ref[hh, :, cols]
                    m_new = jnp.maximum(m_old, m_new)
                    alphas.append(jnp.exp2(m_old - m_new))
                probs.append(jnp.exp2(s - m_new).astype(BF16))
                m_ref[hh, :, cols] = m_new
            for hh in range(2):
                vt = jnp.concatenate(
                    [vt_ref[0, hh * HD:(hh + 1) * HD, j * T + keys.start:j * T + keys.stop],
                     jnp.ones((ONES_ROWS, n_k), BF16)], axis=0)
                pv = jnp.dot(vt, probs[hh], preferred_element_type=F32)
                acc_ref[hh, :, cols] = pv if first else alphas[hh] * acc_ref[hh, :, cols] + pv
        if j == i:
            out_t = jnp.concatenate([acc_ref[hh, 0:HD, :] / acc_ref[hh, HD:HD + 1, :]
                                     for hh in range(2)], axis=0)
            rows = slice(i * T, (i + 1) * T)
            o_ref[0, rows, :] = (out_t.T * gate_ref[0, rows, :]).astype(BF16)

    blocks = [(i, j) for i in range(SEQ // T) for j in range(i + 1)]
    issue_scores(0, *blocks[0])
    for n, (i, j) in enumerate(blocks):
        if n + 1 < len(blocks):
            issue_scores((n + 1) % 2, *blocks[n + 1])
        consume(n % 2, i, j)


def _fox(qa, ka, vt, gate):
    T = FOX_TILE
    W = 2 * FOX_HEAD_DIM
    seq_spec = lambda w: pl.BlockSpec((1, SEQ, w), lambda b, p: (b, 0, p))
    return pl.pallas_call(
        _fox_kernel,
        grid=(BATCH, FOX_HEADS // 2),
        in_specs=[seq_spec(2 * W), seq_spec(2 * W),
                  pl.BlockSpec((1, W, SEQ), lambda b, p: (b, p, 0)),
                  seq_spec(W)],
        out_specs=seq_spec(W),
        out_shape=jax.ShapeDtypeStruct((BATCH, SEQ, MAIN_WIDTH), BF16),
        scratch_shapes=[pltpu.VMEM((2, 2, T, T), F32),
                        pltpu.VMEM((2, 1, T), F32),
                        pltpu.VMEM((2, FOX_HEAD_DIM + ONES_ROWS, T), F32)],
        compiler_params=_params(("parallel", "parallel")),
        name="fox",
    )(qa, ka, vt, gate)


def _memattn_kernel(qm_ref, mem_ref, mgain_ref, wkv_ref, qg_ref, kg_ref, o_ref, kv_ref):
    W, HD = MEM_WIDTH, MEM_HEAD_DIM
    batch = range(MEM_BATCH)
    heads = range(MEM_HEADS)
    ones_rows = jnp.ones((ONES_ROWS, N_MEM), BF16)
    scale = HD ** -0.5 * LOG2E

    for bi in batch:
        mem_n = _rms(mem_ref[bi], mgain_ref[...]).astype(BF16)
        kv_ref[bi] = _mm(mem_n, wkv_ref[...])
    keys, values_t = [], []
    for bi in batch:
        km = _group_rms(kv_ref[bi, :, :W], kg_ref[...], HD).astype(BF16)
        head_of_lane = lax.broadcasted_iota(jnp.int32, km.shape, 1) // HD
        vm_t = kv_ref[bi, :, W:].T.astype(BF16)
        keys.append([jnp.where(head_of_lane == h, km, jnp.zeros_like(km)) for h in heads])
        values_t.append([jnp.concatenate([vm_t[h * HD:(h + 1) * HD, :], ones_rows], axis=0)
                         for h in heads])
    queries = [(_group_rms(qm_ref[bi], qg_ref[...], HD) * scale).astype(BF16) for bi in batch]
    scores = [[lax.dot_general(keys[bi][h], queries[bi], (((1,), (1,)), ((), ())),
                               preferred_element_type=F32) for h in heads] for bi in batch]
    for bi in batch:
        outs = []
        for h in heads:
            s = scores[bi][h]
            p = jnp.exp2(s - jnp.max(s, axis=0, keepdims=True)).astype(BF16)
            r = jnp.dot(values_t[bi][h], p, preferred_element_type=F32)
            outs.append(r[0:HD, :] / r[HD:HD + 1, :])
        o_ref[bi] = jnp.concatenate(outs, axis=0).T.astype(BF16)


def _memattn(layer, qm, mem, mem_gains, w_mem_kv, q_gains_tiled, k_gains_tiled):
    qspec = pl.BlockSpec((MEM_BATCH, SEQ, MEM_WIDTH), lambda b: (b, 0, 0))
    return pl.pallas_call(
        _memattn_kernel,
        grid=(BATCH // MEM_BATCH,),
        in_specs=[qspec,
                  pl.BlockSpec((MEM_BATCH, N_MEM, D_MODEL), lambda b: (b, 0, 0)),
                  _layer((1, D_MODEL), layer), _layer((D_MODEL, 2 * MEM_WIDTH), layer),
                  _layer((1, MEM_WIDTH), layer), _layer((1, MEM_WIDTH), layer)],
        out_specs=qspec,
        out_shape=jax.ShapeDtypeStruct((BATCH, SEQ, MEM_WIDTH), BF16),
        scratch_shapes=[pltpu.VMEM((MEM_BATCH, N_MEM, 2 * MEM_WIDTH), F32)],
        compiler_params=_params(("parallel",)),
        name="memattn",
    )(qm, mem, mem_gains, w_mem_kv, q_gains_tiled, k_gains_tiled)


def kernel(x, mem, ffn1_norm, ffn1_w_gate, ffn1_w_up, ffn1_w_down, mix_norm, mem_norm,
           w_mem_kv, mem_q_gain, mem_k_gain, w_in_a, hgrn_lb_logits, hgrn_o_gain,
           w_in_b, fox_q_gain, kv_norm, w_kv, fox_f_bias, fox_k_gain, w_out,
           ffn2_norm, ffn2_w_gate, ffn2_w_up, ffn2_w_down):
    row = lambda t: t.reshape(1, -1).astype(F32)
    rows = lambda t, reps=1: jnp.tile(t.astype(F32), (1, reps)).reshape(t.shape[0], 1, -1)
    f32 = lambda t: t.astype(F32)
    x = x.reshape(TOKENS, D_MODEL).astype(F32)
    mem = mem.astype(F32)
    lb_logits = hgrn_lb_logits.astype(F32)
    ffn1 = (rows(ffn1_norm), f32(ffn1_w_gate), f32(ffn1_w_up), f32(ffn1_w_down))
    ffn2 = (rows(ffn2_norm), f32(ffn2_w_gate), f32(ffn2_w_up), f32(ffn2_w_down))
    mix_gains = rows(mix_norm)
    o_gains, fox_q_gains = rows(hgrn_o_gain), rows(fox_q_gain, FOX_HEADS)
    mem_args = (rows(mem_norm), f32(w_mem_kv), rows(mem_q_gain, MEM_HEADS),
                rows(mem_k_gain, MEM_HEADS))

    for l in range(DEPTH):
        x = _ffn(l, x, *ffn1)
        if l < N_A_LAYERS:
            qs, lf, iv, sg, qm = _inproj_a(l, x, mix_gains, f32(w_in_a), lb_logits)
            to3 = lambda t: t.reshape(BATCH, SEQ, MAIN_WIDTH)
            main = _hgrn(l, to3(qs), to3(lf), to3(iv), to3(sg), o_gains)
        else:
            qa, gate, qm = _inproj_b(l, x, mix_gains, f32(w_in_b), fox_q_gains,
                                     q_bias.reshape(TOKENS, MAIN_WIDTH))
            main = _fox(qa.reshape(BATCH, SEQ, 2 * MAIN_WIDTH), k_aug, v_t,
                        gate.reshape(BATCH, SEQ, MAIN_WIDTH))
        memo = _memattn(l, qm.reshape(BATCH, SEQ, MEM_WIDTH), mem, *mem_args)
        x = _mix_ffn(l, x, main.reshape(TOKENS, MAIN_WIDTH), memo.reshape(TOKENS, MEM_WIDTH),
                     f32(w_out), *ffn2)
        if l == N_A_LAYERS - 1:
            pad = V7X_LANES - FOX_HEADS
            wf = jnp.pad(f32(w_kv[:, 2 * MAIN_WIDTH:]), ((0, 0), (0, pad)))
            fb = jnp.pad(fox_f_bias.astype(F32), (0, pad)).reshape(1, -1)
            k_aug, v_t, q_bias = _kv(x.reshape(BATCH, SEQ, D_MODEL), row(kv_norm),
                                     f32(w_kv), wf, fb,
                                     row(jnp.tile(fox_k_gain, FOX_HEADS)))
    return x.reshape(BATCH, SEQ, D_MODEL)
```

```python
import functools

import jax
import jax.numpy as jnp
from jax import lax
from jax.experimental import pallas as pl
from jax.experimental.pallas import tpu as pltpu

F32 = jnp.float32
BF16 = jnp.bfloat16

D_MODEL = 1024
BATCH = 8
SEQ = 2048
DEPTH = 4
N_MEM = 256
N_A_LAYERS = DEPTH // 2
MAIN_WIDTH = 768
MEM_WIDTH = 256
HG_HEAD_DIM = 128
HG_HEADS = MAIN_WIDTH // HG_HEAD_DIM
FOX_HEAD_DIM = 64
FOX_HEADS = MAIN_WIDTH // FOX_HEAD_DIM
MEM_HEADS = 4
MEM_HEAD_DIM = MEM_WIDTH // MEM_HEADS
D_FF = 2816
EPS = 1e-6
TOKENS = BATCH * SEQ

V7X_LANES = 128
V7X_VMEM_BYTES = 64 * 1024 * 1024

ROW_TILE = 512
PROJ_ROW_TILE = 1024
FF_CHUNK = 256
FFN_STAGE_SLOTS = 3
HG_CHUNK = 64
HG_SUB = 16
HG_SAFE_DECAY = 60.0
HG_GROUP = 4
HG_HEADS_PER_STEP = 2
FOX_TILE = 512
FOX_HEADS_PER_STEP = 2
ONES_ROWS = 16
MEM_BATCH = 2
KV_TILE = 1024
KV_CUM_SEG = 256
VMEM_LIMIT = 56 * 1024 * 1024


def _rms(x, gain):
    ms = jnp.mean(x * x, axis=-1, keepdims=True)
    return x * lax.rsqrt(ms + EPS) * gain


def _mm(a, w):
    return jnp.dot(a, w.astype(BF16), preferred_element_type=F32)


def _split_dot(x, rhs01, parts):
    acc = None
    rem = x
    for p in range(parts):
        piece = rem.astype(BF16)
        term = jnp.dot(piece, rhs01, preferred_element_type=F32)
        acc = term if acc is None else acc + term
        if p + 1 < parts:
            rem = rem - piece.astype(F32)
    return acc


def _group_ones(width, group):
    r = lax.broadcasted_iota(jnp.int32, (width, width), 0) // group
    c = lax.broadcasted_iota(jnp.int32, (width, width), 1) // group
    return (r == c).astype(BF16)


def _group_rms(x, gain_tiled, group):
    width = x.shape[-1]
    ss = _split_dot(x * x, _group_ones(width, group), 2)
    return x * lax.rsqrt(ss * (1.0 / group) + EPS) * gain_tiled


def _resident(shape, index=None):
    index = (0,) * len(shape) if index is None else index
    return pl.BlockSpec(shape, lambda *_: index, pipeline_mode=pl.Buffered(1))


def _layer(shape, layer, first=0):
    index = (layer, first) + (0,) * (len(shape) - 1)
    return pl.BlockSpec((None,) + tuple(shape), lambda *_: index,
                        pipeline_mode=pl.Buffered(1))


def _params(sem):
    return pltpu.CompilerParams(dimension_semantics=sem, vmem_limit_bytes=VMEM_LIMIT)


def _swiglu_weight_stream(layer, hbm, resident, stage_gu_ref, stage_d_ref, sem):
    wg_hbm, wu_hbm, wd_hbm = hbm
    wg_ref, wu_ref, wd_ref = resident

    def copies(c):
        slot = c % FFN_STAGE_SLOTS
        span = pl.ds(c * FF_CHUNK, FF_CHUNK)
        return (pltpu.make_async_copy(wg_hbm.at[layer, :, span], stage_gu_ref.at[0, slot],
                                      sem.at[0, slot]),
                pltpu.make_async_copy(wu_hbm.at[layer, :, span], stage_gu_ref.at[1, slot],
                                      sem.at[1, slot]),
                pltpu.make_async_copy(wd_hbm.at[layer, span, :], stage_d_ref.at[slot],
                                      sem.at[2, slot]))

    def start(c):
        for copy in copies(c):
            copy.start()

    def fetch(c):
        slot = c % FFN_STAGE_SLOTS
        sl = slice(c * FF_CHUNK, (c + 1) * FF_CHUNK)
        for copy in copies(c):
            copy.wait()
        wg_ref[:, sl] = stage_gu_ref[0, slot].astype(BF16)
        wu_ref[:, sl] = stage_gu_ref[1, slot].astype(BF16)
        wd_ref[sl, :] = stage_d_ref[slot].astype(BF16)
        if c + FFN_STAGE_SLOTS < D_FF // FF_CHUNK:
            start(c + FFN_STAGE_SLOTS)

    return start, fetch


def _swiglu_rows(x, gain_ref, resident, h_ref, fetch):
    wg_ref, wu_ref, wd_ref = resident
    n = _rms(x, gain_ref[...]).astype(BF16)
    for c in range(D_FF // FF_CHUNK):
        if fetch is not None:
            fetch(c)
        sl = slice(c * FF_CHUNK, (c + 1) * FF_CHUNK)
        g = jnp.dot(n, wg_ref[:, sl], preferred_element_type=F32)
        u = jnp.dot(n, wu_ref[:, sl], preferred_element_type=F32)
        h_ref[:, sl] = (g * jax.nn.sigmoid(g) * u).astype(BF16)
    y = jnp.dot(h_ref[...], wd_ref[...], preferred_element_type=F32)
    return x + 0.5 * y


def _swiglu_steps(layer, read_x, gain_ref, hbm, o_ref, scratch):
    h_ref, wg_ref, wu_ref, wd_ref, stage_gu_ref, stage_d_ref, sem = scratch
    resident = (wg_ref, wu_ref, wd_ref)
    start, fetch = _swiglu_weight_stream(layer, hbm, resident, stage_gu_ref, stage_d_ref, sem)

    @pl.when(pl.program_id(0) == 0)
    def _():
        for c in range(FFN_STAGE_SLOTS):
            start(c)
        o_ref[...] = _swiglu_rows(read_x(), gain_ref, resident, h_ref, fetch)

    @pl.when(pl.program_id(0) > 0)
    def _():
        o_ref[...] = _swiglu_rows(read_x(), gain_ref, resident, h_ref, None)


def _ffn_kernel(layer, x_ref, gain_ref, wg_hbm, wu_hbm, wd_hbm, o_ref, *scratch):
    _swiglu_steps(layer, lambda: x_ref[...], gain_ref, (wg_hbm, wu_hbm, wd_hbm), o_ref, scratch)


def _mix_ffn_kernel(layer, x_ref, main_ref, memo_ref, wo_main_ref, wo_mem_ref,
                    gain_ref, wg_hbm, wu_hbm, wd_hbm, o_ref, *scratch):
    def mixed_x():
        return (x_ref[...]
                + _mm(main_ref[...], wo_main_ref[...])
                + _mm(memo_ref[...], wo_mem_ref[...]))
    _swiglu_steps(layer, mixed_x, gain_ref, (wg_hbm, wu_hbm, wd_hbm), o_ref, scratch)


def _row_spec(width):
    return pl.BlockSpec((ROW_TILE, width), lambda i: (i, 0))


def _proj_spec(width):
    return pl.BlockSpec((PROJ_ROW_TILE, width), lambda i: (i, 0))


_IN_HBM = pl.BlockSpec(memory_space=pl.ANY)
_SWIGLU_SCRATCH = [
    pltpu.VMEM((ROW_TILE, D_FF), BF16),
    pltpu.VMEM((D_MODEL, D_FF), BF16),
    pltpu.VMEM((D_MODEL, D_FF), BF16),
    pltpu.VMEM((D_FF, D_MODEL), BF16),
    pltpu.VMEM((2, FFN_STAGE_SLOTS, D_MODEL, FF_CHUNK), F32),
    pltpu.VMEM((FFN_STAGE_SLOTS, FF_CHUNK, D_MODEL), F32),
    pltpu.SemaphoreType.DMA((3, FFN_STAGE_SLOTS)),
]


def _ffn(layer, x, gains, wg, wu, wd):
    return pl.pallas_call(
        functools.partial(_ffn_kernel, layer),
        grid=(TOKENS // ROW_TILE,),
        in_specs=[_row_spec(D_MODEL), _layer((1, D_MODEL), layer), _IN_HBM, _IN_HBM, _IN_HBM],
        out_specs=_row_spec(D_MODEL),
        out_shape=jax.ShapeDtypeStruct((TOKENS, D_MODEL), F32),
        scratch_shapes=_SWIGLU_SCRATCH,
        compiler_params=_params(("arbitrary",)),
        name="ffn",
    )(x, gains, wg, wu, wd)


def _mix_ffn(layer, x, main, memo, w_out, gains, wg, wu, wd):
    assert MAIN_WIDTH % MEM_WIDTH == 0
    return pl.pallas_call(
        functools.partial(_mix_ffn_kernel, layer),
        grid=(TOKENS // ROW_TILE,),
        in_specs=[_row_spec(D_MODEL), _row_spec(MAIN_WIDTH), _row_spec(MEM_WIDTH),
                  _layer((MAIN_WIDTH, D_MODEL), layer),
                  _layer((MEM_WIDTH, D_MODEL), layer, MAIN_WIDTH // MEM_WIDTH),
                  _layer((1, D_MODEL), layer), _IN_HBM, _IN_HBM, _IN_HBM],
        out_specs=_row_spec(D_MODEL),
        out_shape=jax.ShapeDtypeStruct((TOKENS, D_MODEL), F32),
        scratch_shapes=_SWIGLU_SCRATCH,
        compiler_params=_params(("arbitrary",)),
        name="mix_ffn",
    )(x, main, memo, w_out, w_out, gains, wg, wu, wd)


def _inproj_a_kernel(layer, x_ref, gain_ref, w_ref, lbl_ref,
                     q_ref, lf_ref, i_ref, g_ref, qm_ref):
    n = _rms(x_ref[...], gain_ref[...]).astype(BF16)
    rows = [lbl_ref[r:r + 1, :] for r in range(N_A_LAYERS)]
    top = functools.reduce(jnp.maximum, rows)
    exps = [jnp.exp(r - top) for r in rows]
    total = functools.reduce(jnp.add, exps)
    lb = jnp.zeros_like(top)
    for r in range(1, layer + 1):
        lb = lb + exps[r] / total

    w = MAIN_WIDTH
    q_raw = _mm(n, w_ref[:, 0:w])
    q_ref[...] = q_raw * jax.nn.sigmoid(q_raw)
    f_raw = _mm(n, w_ref[:, w:2 * w])
    lf_ref[...] = jnp.log(lb + (1.0 - lb) * jax.nn.sigmoid(f_raw))
    i_ref[...] = _mm(n, w_ref[:, 2 * w:3 * w]).astype(BF16)
    g_raw = _mm(n, w_ref[:, 3 * w:4 * w])
    g_ref[...] = g_raw * jax.nn.sigmoid(g_raw)
    qm_ref[...] = _mm(n, w_ref[:, 4 * w:])


def _inproj_a(layer, x, gains, w_in, lb_logits):
    a_in = 4 * MAIN_WIDTH + MEM_WIDTH
    return pl.pallas_call(
        functools.partial(_inproj_a_kernel, layer),
        grid=(TOKENS // PROJ_ROW_TILE,),
        in_specs=[_proj_spec(D_MODEL), _layer((1, D_MODEL), layer),
                  _layer((D_MODEL, a_in), layer), _resident((N_A_LAYERS, MAIN_WIDTH))],
        out_specs=[_proj_spec(MAIN_WIDTH), _proj_spec(MAIN_WIDTH), _proj_spec(MAIN_WIDTH),
                   _proj_spec(MAIN_WIDTH), _proj_spec(MEM_WIDTH)],
        out_shape=[jax.ShapeDtypeStruct((TOKENS, MAIN_WIDTH), F32),
                   jax.ShapeDtypeStruct((TOKENS, MAIN_WIDTH), F32),
                   jax.ShapeDtypeStruct((TOKENS, MAIN_WIDTH), BF16),
                   jax.ShapeDtypeStruct((TOKENS, MAIN_WIDTH), F32),
                   jax.ShapeDtypeStruct((TOKENS, MEM_WIDTH), F32)],
        compiler_params=_params(("parallel",)),
        name="inproj_a",
    )(x, gains, w_in, lb_logits)


def _inproj_b_kernel(x_ref, gain_ref, w_ref, qg_ref, qbias_ref, qa_ref, gate_ref, qm_ref):
    n = _rms(x_ref[...], gain_ref[...]).astype(BF16)
    w = MAIN_WIDTH
    scale = FOX_HEAD_DIM ** -0.5 * LOG2E
    pair_w = 2 * FOX_HEAD_DIM
    q_raw = [_mm(n, w_ref[:, c * 256:(c + 1) * 256]) for c in range(w // 256)]
    gate = _mm(n, w_ref[:, w:2 * w])
    qm_ref[...] = _mm(n, w_ref[:, 2 * w:])
    for c in range(w // 256):
        sl = slice(c * 256, (c + 1) * 256)
        q_n = (_group_rms(q_raw[c], qg_ref[:, sl], FOX_HEAD_DIM) * scale).astype(BF16)
        for pp in range(256 // pair_w):
            lo = c * 256 + pp * pair_w
            for hh in range(2):
                dst = 2 * lo + hh * pair_w
                qa_ref[:, dst:dst + pair_w] = _with_bias_lanes(
                    q_n[:, pp * pair_w:(pp + 1) * pair_w], qbias_ref[:, lo:lo + pair_w], hh)
    gate_ref[...] = jax.nn.sigmoid(gate)


def _inproj_b(layer, x, gains, w_in, q_gain_tiled, q_bias):
    b_in = 2 * MAIN_WIDTH + MEM_WIDTH
    j = layer - N_A_LAYERS
    return pl.pallas_call(
        _inproj_b_kernel,
        grid=(TOKENS // PROJ_ROW_TILE,),
        in_specs=[_proj_spec(D_MODEL), _layer((1, D_MODEL), layer),
                  _layer((D_MODEL, b_in), j), _layer((1, MAIN_WIDTH), j),
                  _proj_spec(MAIN_WIDTH)],
        out_specs=[_proj_spec(2 * MAIN_WIDTH), _proj_spec(MAIN_WIDTH), _proj_spec(MEM_WIDTH)],
        out_shape=[jax.ShapeDtypeStruct((TOKENS, 2 * MAIN_WIDTH), BF16),
                   jax.ShapeDtypeStruct((TOKENS, MAIN_WIDTH), F32),
                   jax.ShapeDtypeStruct((TOKENS, MEM_WIDTH), F32)],
        compiler_params=_params(("parallel",)),
        name="inproj_b",
    )(x, gains, w_in, q_gain_tiled, q_bias)


def _hgrn_kernel(q_ref, lf_ref, v_ref, g_ref, gain_ref, o_ref,
                 cpad_ref, kpad_ref, vpad_ref):
    C, SB, HD = HG_CHUNK, HG_SUB, HG_HEAD_DIM
    n_sub = C // SB
    r_i = lax.broadcasted_iota(jnp.int32, (C, C), 0)
    c_i = lax.broadcasted_iota(jnp.int32, (C, C), 1)
    tril = (r_i >= c_i).astype(BF16)
    gain = gain_ref[...]

    def intra_exact(q, k, v16, c):
        outs = [jnp.zeros((SB, HD), F32)]
        for b in range(1, n_sub):
            lo = b * SB
            ref = c[lo - 1:lo, :]
            qb = (q[lo:lo + SB, :] * jnp.exp(c[lo:lo + SB, :] - ref)).astype(BF16)
            kb = (k[0:lo, :] * jnp.exp(ref - c[0:lo, :])).astype(BF16)
            p_b = lax.dot_general(qb, kb, (((1,), (1,)), ((), ())),
                                  preferred_element_type=F32)
            outs.append(jnp.dot(p_b.astype(BF16), v16[0:lo, :], preferred_element_type=F32))
        o = jnp.concatenate(outs, axis=0)
        ones = jnp.ones((HD, HD), BF16)
        row_in_block = lax.broadcasted_iota(jnp.int32, (C, HD), 0) % SB
        cpad_ref[SB:SB + C, :] = c
        kpad_ref[SB:SB + C, :] = k
        vpad_ref[SB:SB + C, :] = v16.astype(F32)
        for d in range(SB):
            cs = cpad_ref[SB - d:SB - d + C, :]
            ks = kpad_ref[SB - d:SB - d + C, :]
            vs = vpad_ref[SB - d:SB - d + C, :]
            wgt = jnp.where(row_in_block >= d, q * ks * jnp.exp(c - cs), 0.0)
            s = jnp.dot(wgt.astype(BF16), ones, preferred_element_type=F32)
            o = o + s * vs
        return o

    def chunk(n, st, intra, lanes):
        r0 = pl.multiple_of(n * C, C)
        lf = lf_ref[0, pl.ds(r0, C), lanes]
        q = q_ref[0, pl.ds(r0, C), lanes]
        v16 = v_ref[0, pl.ds(r0, C), lanes]
        k = 1.0 - jnp.exp(lf)
        c = None
        rem = lf
        for p in range(3):
            piece = rem.astype(BF16)
            term = jnp.dot(tril, piece, preferred_element_type=F32)
            c = term if c is None else c + term
            rem = rem - piece.astype(F32)
        c_end = c[C - 1:C, :]

        qe = (q * jnp.exp(c)).astype(BF16)
        o = lax.dot_general(qe, st.astype(BF16), (((1,), (1,)), ((), ())),
                            preferred_element_type=F32)
        kd = (k * jnp.exp(c_end - c)).astype(BF16)
        upd = lax.dot_general(v16, kd, (((0,), (0,)), ((), ())),
                              preferred_element_type=F32)
        o = o + intra(q, k, v16, c)
        o_ref[0, pl.ds(r0, C), lanes] = (
            _rms(o, gain) * g_ref[0, pl.ds(r0, C), lanes]).astype(BF16)
        return st * jnp.exp(c_end) + upd

    G = HG_GROUP
    R = G * C
    gr = lax.broadcasted_iota(jnp.int32, (R, R), 0)
    gc = lax.broadcasted_iota(jnp.int32, (R, R), 1)
    tril_group = ((gr // C == gc // C) & (gr >= gc)).astype(BF16)
    causal = r_i >= c_i

    def stage_cumsum(i):
        lf = lf_ref[0, i * R:(i + 1) * R, :]
        c = None
        rem = lf
        for p in range(3):
            piece = rem.astype(BF16)
            term = jnp.dot(tril_group, piece, preferred_element_type=F32)
            c = term if c is None else c + term
            rem = rem - piece.astype(F32)
        return lf, c

    def stage_scores(i, lanes, lf, c):
        lf, c = lf[:, lanes], c[:, lanes]
        q = q_ref[0, i * R:(i + 1) * R, lanes]
        v16 = v_ref[0, i * R:(i + 1) * R, lanes]
        k = 1.0 - jnp.exp(lf)
        lhs, rhs, c_ends = [], [], []
        for g in range(G):
            cg = c[g * C:(g + 1) * C, :]
            qg = q[g * C:(g + 1) * C, :]
            kg = k[g * C:(g + 1) * C, :]
            refs = [jnp.zeros((1, HD), F32)] + [cg[b * SB - 1:b * SB, :] for b in range(1, n_sub)]
            own_ref = jnp.concatenate([jnp.broadcast_to(r, (SB, HD)) for r in refs], axis=0)
            k_own = (kg * jnp.exp(own_ref - cg)).astype(BF16)
            lhs_cols, rhs_cols = [], []
            for b in range(n_sub):
                lo = b * SB
                q_b = (qg[lo:, :] * jnp.exp(cg[lo:, :] - refs[b])).astype(BF16)
                k_b = k_own[lo:lo + SB, :]
                above = [jnp.zeros((lo, HD), BF16)] if lo else []
                below = [jnp.zeros((C - lo - SB, HD), BF16)] if lo + SB < C else []
                lhs_cols.append(jnp.concatenate(above + [q_b], axis=0))
                rhs_cols.append(jnp.concatenate(above + [k_b] + below, axis=0))
            lhs.append(jnp.concatenate(lhs_cols, axis=1))
            rhs.append(jnp.concatenate(rhs_cols, axis=1))
            c_ends.append(cg[C - 1:C, :])
        c_end_rows = jnp.concatenate([jnp.broadcast_to(e, (C, HD)) for e in c_ends], axis=0)
        qe = (q * jnp.exp(c)).astype(BF16)
        kd = (k * jnp.exp(c_end_rows - c)).astype(BF16)

        scores = [lax.dot_general(lhs[g], rhs[g], (((1,), (1,)), ((), ())),
                                  preferred_element_type=F32) for g in range(G)]
        upds = [lax.dot_general(v16[g * C:(g + 1) * C, :], kd[g * C:(g + 1) * C, :],
                                (((0,), (0,)), ((), ())), preferred_element_type=F32)
                for g in range(G)]
        return scores, upds, qe, c_ends

    def stage_output(i, lanes, prepared, st):
        scores, upds, qe, c_ends = prepared
        v16 = v_ref[0, i * R:(i + 1) * R, lanes]
        intra = [jnp.dot(jnp.where(causal, scores[g], 0.0).astype(BF16),
                         v16[g * C:(g + 1) * C, :], preferred_element_type=F32)
                 for g in range(G)]
        outs = []
        for g in range(G):
            inter = lax.dot_general(qe[g * C:(g + 1) * C, :], st.astype(BF16),
                                    (((1,), (1,)), ((), ())), preferred_element_type=F32)
            outs.append(inter + intra[g])
            st = st * jnp.exp(c_ends[g]) + upds[g]
        o = jnp.concatenate(outs, axis=0)
        rows = slice(i * R, (i + 1) * R)
        o_ref[0, rows, lanes] = (_rms(o, gain) * g_ref[0, rows, lanes]).astype(BF16)
        return st

    head_lanes = [slice(hh * HD, (hh + 1) * HD) for hh in range(HG_HEADS_PER_STEP)]

    def run_factored():
        n_groups = SEQ // R
        first = stage_cumsum(0)
        prepared = [stage_scores(0, lanes, *first) for lanes in head_lanes]
        states = [jnp.zeros((HD, HD), F32) for _ in head_lanes]
        for i in range(n_groups):
            if i + 1 < n_groups:
                ahead = stage_cumsum(i + 1)
            states = [stage_output(i, lanes, prepared[hh], states[hh])
                      for hh, lanes in enumerate(head_lanes)]
            if i + 1 < n_groups:
                prepared = [stage_scores(i + 1, lanes, *ahead) for lanes in head_lanes]

    def run_exact():
        for lanes in head_lanes:
            def body(n, st, lanes=lanes):
                return chunk(n, st, intra_exact, lanes)
            lax.fori_loop(0, SEQ // C, body, jnp.zeros((HD, HD), F32))

    blk = lax.broadcasted_iota(jnp.int32, (SEQ // SB, SEQ), 0)
    tok = lax.broadcasted_iota(jnp.int32, (SEQ // SB, SEQ), 1) // SB
    block_decay = jnp.dot((blk == tok).astype(BF16), lf_ref[0].astype(BF16),
                          preferred_element_type=F32)
    mild = jnp.min(block_decay) >= -HG_SAFE_DECAY

    @pl.when(mild)
    def _():
        run_factored()

    @pl.when(jnp.logical_not(mild))
    def _():
        zpad = jnp.zeros((SB, HD), F32)
        cpad_ref[0:SB, :] = zpad
        kpad_ref[0:SB, :] = zpad
        vpad_ref[0:SB, :] = zpad
        run_exact()


def _hgrn(layer, q, lf, v, g, o_gains):
    spec = pl.BlockSpec((1, SEQ, HG_HEADS_PER_STEP * HG_HEAD_DIM), lambda b, h: (b, 0, h))
    return pl.pallas_call(
        _hgrn_kernel,
        grid=(BATCH, HG_HEADS // HG_HEADS_PER_STEP),
        in_specs=[spec, spec, spec, spec, _layer((1, HG_HEAD_DIM), layer)],
        out_specs=spec,
        out_shape=jax.ShapeDtypeStruct((BATCH, SEQ, MAIN_WIDTH), BF16),
        scratch_shapes=[pltpu.VMEM((HG_SUB + HG_CHUNK, HG_HEAD_DIM), F32),
                        pltpu.VMEM((HG_SUB + HG_CHUNK, HG_HEAD_DIM), F32),
                        pltpu.VMEM((HG_SUB + HG_CHUNK, HG_HEAD_DIM), F32)],
        compiler_params=_params(("parallel", "parallel")),
        name="hgrn",
    )(q, lf, v, g, o_gains)


FOX_BIAS_PIECES = 3
FOX_PIECE_STRIDE = 16
LOG2E = 1.4426950408889634


def _bias_slot_layout():
    lane = lax.broadcasted_iota(jnp.int32, (V7X_LANES, MAIN_WIDTH), 1)
    pair = lane // (2 * FOX_HEAD_DIM)
    half = (lane // FOX_HEAD_DIM) % 2
    head_here = 2 * pair + (1 - half)
    slot = lane % FOX_HEAD_DIM
    return head_here, slot


def _place_bias(packed):
    head_here, slot = _bias_slot_layout()
    src = lax.broadcasted_iota(jnp.int32, (V7X_LANES, MAIN_WIDTH), 0)
    src_piece, src_head = src // FOX_PIECE_STRIDE, src % FOX_PIECE_STRIDE
    hit = (src_piece < FOX_BIAS_PIECES) & (src_head == head_here)
    place = jnp.concatenate([(hit & (slot == src_piece)).astype(BF16),
                             (hit & (slot == src_piece + FOX_BIAS_PIECES)).astype(BF16)],
                            axis=1)
    return jnp.dot(packed, place, preferred_element_type=F32)


def _with_bias_lanes(pair_vals, pair_bias, head_in_pair):
    own = (lax.broadcasted_iota(jnp.int32, pair_vals.shape, 1) // FOX_HEAD_DIM) == head_in_pair
    return jnp.where(own, pair_vals, pair_bias)


def _kv_kernel(x_ref, gain_ref, wk_ref, wv_ref, wf_ref, fb_ref, kg_ref,
               ka_ref, vt_ref, qbias_ref, carry_ref, v_ref):
    @pl.when(pl.program_id(1) == 0)
    def _():
        carry_ref[...] = jnp.zeros_like(carry_ref)

    n = _rms(x_ref[0], gain_ref[...]).astype(BF16)

    z = _mm(n, wf_ref[...]) + fb_ref[...]
    v_ref[...] = _mm(n, wv_ref[...])
    k_raw = [_mm(n, wk_ref[:, c * 256:(c + 1) * 256]) for c in range(MAIN_WIDTH // 256)]
    log_f = jnp.minimum(z, 0.0) - jnp.log1p(jnp.exp(-jnp.abs(z)))
    seg = KV_CUM_SEG
    tril = (lax.broadcasted_iota(jnp.int32, (seg, seg), 0)
            >= lax.broadcasted_iota(jnp.int32, (seg, seg), 1)).astype(BF16)
    local = []
    for sgi in range(KV_TILE // seg):
        rem = log_f[sgi * seg:(sgi + 1) * seg, :]
        acc = None
        for p in range(3):
            piece = rem.astype(BF16)
            term = jnp.dot(tril, piece, preferred_element_type=F32)
            acc = term if acc is None else acc + term
            rem = rem - piece.astype(F32)
        local.append(acc)
    last = carry_ref[...]
    parts = []
    for acc in local:
        parts.append(acc + last)
        last = parts[-1][seg - 1:seg, :]
    carry_ref[...] = last
    cum = jnp.concatenate(parts, axis=0)

    lane = lax.broadcasted_iota(jnp.int32, cum.shape, 1)
    rem = jnp.where(lane < FOX_HEADS, cum * LOG2E, 0.0)
    packed = None
    for p in range(FOX_BIAS_PIECES):
        piece = rem.astype(BF16).astype(F32)
        moved = pltpu.roll(piece, p * FOX_PIECE_STRIDE, axis=1) if p else piece
        packed = moved if packed is None else packed + moved
        rem = rem - piece
    placed = _place_bias(packed.astype(BF16))
    _, slot = _bias_slot_layout()
    slot_row = slot[0:1, :]
    n_p = FOX_BIAS_PIECES
    k_bias = (placed[:, :MAIN_WIDTH]
              + ((slot_row >= n_p) & (slot_row < 2 * n_p)).astype(F32)).astype(BF16)
    qbias_ref[0] = (placed[:, MAIN_WIDTH:] - (slot_row < n_p).astype(F32)).astype(BF16)

    pair_w = 2 * FOX_HEAD_DIM
    for c in range(MAIN_WIDTH // 256):
        sl = slice(c * 256, (c + 1) * 256)
        k_n = _group_rms(k_raw[c], kg_ref[:, sl], FOX_HEAD_DIM).astype(BF16)
        for pp in range(256 // pair_w):
            lo = c * 256 + pp * pair_w
            for hh in range(2):
                dst = 2 * lo + hh * pair_w
                ka_ref[0, :, dst:dst + pair_w] = _with_bias_lanes(
                    k_n[:, pp * pair_w:(pp + 1) * pair_w], k_bias[:, lo:lo + pair_w], hh)
    vt_ref[0] = v_ref[...].T.astype(BF16)


def _kv(x, gain, w_kv, wf, f_bias, k_gain_tiled):
    row = lambda w: pl.BlockSpec((1, KV_TILE, w), lambda b, i: (b, i, 0))
    return pl.pallas_call(
        _kv_kernel,
        grid=(BATCH, SEQ // KV_TILE),
        in_specs=[row(D_MODEL), _resident((1, D_MODEL)),
                  _resident((D_MODEL, MAIN_WIDTH), (0, 0)),
                  _resident((D_MODEL, MAIN_WIDTH), (0, 1)), _resident((D_MODEL, V7X_LANES)),
                  _resident((1, V7X_LANES)), _resident((1, MAIN_WIDTH))],
        out_specs=[row(2 * MAIN_WIDTH),
                   pl.BlockSpec((1, MAIN_WIDTH, KV_TILE), lambda b, i: (b, 0, i)),
                   row(MAIN_WIDTH)],
        out_shape=[jax.ShapeDtypeStruct((BATCH, SEQ, 2 * MAIN_WIDTH), BF16),
                   jax.ShapeDtypeStruct((BATCH, MAIN_WIDTH, SEQ), BF16),
                   jax.ShapeDtypeStruct((BATCH, SEQ, MAIN_WIDTH), BF16)],
        scratch_shapes=[pltpu.VMEM((1, V7X_LANES), F32),
                        pltpu.VMEM((KV_TILE, MAIN_WIDTH), F32)],
        compiler_params=_params(("parallel", "arbitrary")),
        name="kv",
    )(x, gain, w_kv, w_kv, wf, f_bias, k_gain_tiled)


def _fox_kernel(qa_ref, ka_ref, vt_ref, gate_ref, o_ref, s_ref, m_ref, acc_ref):
    T, HD = FOX_TILE, FOX_HEAD_DIM
    W = 2 * HD
    H = T // 2
    NH = FOX_HEADS_PER_STEP

    def pieces(i, j):
        if j == i:
            return [(slice(0, H), slice(0, T)), (slice(H, T), slice(H, T))]
        return [(slice(0, T), slice(0, T))]

    def issue_scores(slot, i, j):
        for hh in range(NH):
            for keys, cols in pieces(i, j):
                s_ref[slot, hh, keys, cols] = lax.dot_general(
                    ka_ref[0, j * T + keys.start:j * T + keys.stop, hh * W:(hh + 1) * W],
                    qa_ref[0, i * T + cols.start:i * T + cols.stop, hh * W:(hh + 1) * W],
                    (((1,), (1,)), ((), ())), preferred_element_type=F32)

    def consume(slot, i, j):
        for n_piece, (keys, cols) in enumerate(pieces(i, j)):
            n_k, n_q = keys.stop - keys.start, cols.stop - cols.start
            first = j == 0 and n_piece == 0
            probs, alphas = [], []
            for hh in range(NH):
                s = s_ref[slot, hh, keys, cols]
                if j == i:
                    key_pos = lax.broadcasted_iota(jnp.int32, (n_k, n_q), 0) + keys.start
                    query_pos = lax.broadcasted_iota(jnp.int32, (n_k, n_q), 1) + cols.start
                    s = jnp.where(key_pos <= query_pos, s, -jnp.inf)
                m_new = jnp.max(s, axis=0, keepdims=True)
                if not first:
                    m_old = m_ref[hh, :, cols]
                    m_new = jnp.maximum(m_old, m_new)
                    alphas.append(jnp.exp2(m_old - m_new))
                probs.append(jnp.exp2(s - m_new).astype(BF16))
                m_ref[hh, :, cols] = m_new
            for hh in range(NH):
                vt = jnp.concatenate(
                    [vt_ref[0, hh * HD:(hh + 1) * HD, j * T + keys.start:j * T + keys.stop],
                     jnp.ones((ONES_ROWS, n_k), BF16)], axis=0)
                pv = jnp.dot(vt, probs[hh], preferred_element_type=F32)
                acc_ref[hh, :, cols] = pv if first else alphas[hh] * acc_ref[hh, :, cols] + pv
        if j == i:
            out_t = jnp.concatenate([acc_ref[hh, 0:HD, :] / acc_ref[hh, HD:HD + 1, :]
                                     for hh in range(NH)], axis=0)
            rows = slice(i * T, (i + 1) * T)
            o_ref[0, rows, :] = (out_t.T * gate_ref[0, rows, :]).astype(BF16)

    blocks = [(i, j) for i in range(SEQ // T) for j in range(i + 1)]
    issue_scores(0, *blocks[0])
    for n, (i, j) in enumerate(blocks):
        if n + 1 < len(blocks):
            issue_scores((n + 1) % 2, *blocks[n + 1])
        consume(n % 2, i, j)


def _fox(qa, ka, vt, gate):
    T = FOX_TILE
    W = 2 * FOX_HEAD_DIM
    NH = FOX_HEADS_PER_STEP
    seq_spec = lambda w: pl.BlockSpec((1, SEQ, w), lambda b, p: (b, 0, p))
    return pl.pallas_call(
        _fox_kernel,
        grid=(BATCH, FOX_HEADS // NH),
        in_specs=[seq_spec(NH * W), seq_spec(NH * W),
                  pl.BlockSpec((1, NH * FOX_HEAD_DIM, SEQ), lambda b, p: (b, p, 0)),
                  seq_spec(NH * FOX_HEAD_DIM)],
        out_specs=seq_spec(NH * FOX_HEAD_DIM),
        out_shape=jax.ShapeDtypeStruct((BATCH, SEQ, MAIN_WIDTH), BF16),
        scratch_shapes=[pltpu.VMEM((2, NH, T, T), F32),
                        pltpu.VMEM((NH, 1, T), F32),
                        pltpu.VMEM((NH, FOX_HEAD_DIM + ONES_ROWS, T), F32)],
        compiler_params=_params(("parallel", "parallel")),
        name="fox",
    )(qa, ka, vt, gate)


def _memattn_kernel(qm_ref, mem_ref, mgain_ref, wkv_ref, qg_ref, kg_ref, o_ref, kv_ref):
    W, HD = MEM_WIDTH, MEM_HEAD_DIM
    batch = range(MEM_BATCH)
    heads = range(MEM_HEADS)
    ones_rows = jnp.ones((ONES_ROWS, N_MEM), BF16)
    scale = HD ** -0.5 * LOG2E

    for bi in batch:
        mem_n = _rms(mem_ref[bi], mgain_ref[...]).astype(BF16)
        kv_ref[bi] = _mm(mem_n, wkv_ref[...])
    keys, values_t = [], []
    for bi in batch:
        km = _group_rms(kv_ref[bi, :, :W], kg_ref[...], HD).astype(BF16)
        head_of_lane = lax.broadcasted_iota(jnp.int32, km.shape, 1) // HD
        vm_t = kv_ref[bi, :, W:].T.astype(BF16)
        keys.append([jnp.where(head_of_lane == h, km, jnp.zeros_like(km)) for h in heads])
        values_t.append([jnp.concatenate([vm_t[h * HD:(h + 1) * HD, :], ones_rows], axis=0)
                         for h in heads])
    queries = [(_group_rms(qm_ref[bi], qg_ref[...], HD) * scale).astype(BF16) for bi in batch]
    scores = [[lax.dot_general(keys[bi][h], queries[bi], (((1,), (1,)), ((), ())),
                               preferred_element_type=F32) for h in heads] for bi in batch]
    for bi in batch:
        outs = []
        for h in heads:
            s = scores[bi][h]
            p = jnp.exp2(s - jnp.max(s, axis=0, keepdims=True)).astype(BF16)
            r = jnp.dot(values_t[bi][h], p, preferred_element_type=F32)
            outs.append(r[0:HD, :] / r[HD:HD + 1, :])
        o_ref[bi] = jnp.concatenate(outs, axis=0).T.astype(BF16)


def _memattn(layer, qm, mem, mem_gains, w_mem_kv, q_gains_tiled, k_gains_tiled):
    qspec = pl.BlockSpec((MEM_BATCH, SEQ, MEM_WIDTH), lambda b: (b, 0, 0))
    return pl.pallas_call(
        _memattn_kernel,
        grid=(BATCH // MEM_BATCH,),
        in_specs=[qspec,
                  pl.BlockSpec((MEM_BATCH, N_MEM, D_MODEL), lambda b: (b, 0, 0)),
                  _layer((1, D_MODEL), layer), _layer((D_MODEL, 2 * MEM_WIDTH), layer),
                  _layer((1, MEM_WIDTH), layer), _layer((1, MEM_WIDTH), layer)],
        out_specs=qspec,
        out_shape=jax.ShapeDtypeStruct((BATCH, SEQ, MEM_WIDTH), BF16),
        scratch_shapes=[pltpu.VMEM((MEM_BATCH, N_MEM, 2 * MEM_WIDTH), F32)],
        compiler_params=_params(("parallel",)),
        name="memattn",
    )(qm, mem, mem_gains, w_mem_kv, q_gains_tiled, k_gains_tiled)


def kernel(x, mem, ffn1_norm, ffn1_w_gate, ffn1_w_up, ffn1_w_down, mix_norm, mem_norm,
           w_mem_kv, mem_q_gain, mem_k_gain, w_in_a, hgrn_lb_logits, hgrn_o_gain,
           w_in_b, fox_q_gain, kv_norm, w_kv, fox_f_bias, fox_k_gain, w_out,
           ffn2_norm, ffn2_w_gate, ffn2_w_up, ffn2_w_down):
    row = lambda t: t.reshape(1, -1).astype(F32)
    rows = lambda t, reps=1: jnp.tile(t.astype(F32), (1, reps)).reshape(t.shape[0], 1, -1)
    f32 = lambda t: t.astype(F32)
    x = x.reshape(TOKENS, D_MODEL).astype(F32)
    mem = mem.astype(F32)
    lb_logits = hgrn_lb_logits.astype(F32)
    ffn1 = (rows(ffn1_norm), f32(ffn1_w_gate), f32(ffn1_w_up), f32(ffn1_w_down))
    ffn2 = (rows(ffn2_norm), f32(ffn2_w_gate), f32(ffn2_w_up), f32(ffn2_w_down))
    mix_gains = rows(mix_norm)
    o_gains, fox_q_gains = rows(hgrn_o_gain), rows(fox_q_gain, FOX_HEADS)
    mem_args = (rows(mem_norm), f32(w_mem_kv), rows(mem_q_gain, MEM_HEADS),
                rows(mem_k_gain, MEM_HEADS))

    for l in range(DEPTH):
        x = _ffn(l, x, *ffn1)
        if l < N_A_LAYERS:
            qs, lf, iv, sg, qm = _inproj_a(l, x, mix_gains, f32(w_in_a), lb_logits)
            to3 = lambda t: t.reshape(BATCH, SEQ, MAIN_WIDTH)
            main = _hgrn(l, to3(qs), to3(lf), to3(iv), to3(sg), o_gains)
        else:
            qa, gate, qm = _inproj_b(l, x, mix_gains, f32(w_in_b), fox_q_gains,
                                     q_bias.reshape(TOKENS, MAIN_WIDTH))
            main = _fox(qa.reshape(BATCH, SEQ, 2 * MAIN_WIDTH), k_aug, v_t,
                        gate.reshape(BATCH, SEQ, MAIN_WIDTH))
        memo = _memattn(l, qm.reshape(BATCH, SEQ, MEM_WIDTH), mem, *mem_args)
        x = _mix_ffn(l, x, main.reshape(TOKENS, MAIN_WIDTH), memo.reshape(TOKENS, MEM_WIDTH),
                     f32(w_out), *ffn2)
        if l == N_A_LAYERS - 1:
            pad = V7X_LANES - FOX_HEADS
            wf = jnp.pad(f32(w_kv[:, 2 * MAIN_WIDTH:]), ((0, 0), (0, pad)))
            fb = jnp.pad(fox_f_bias.astype(F32), (0, pad)).reshape(1, -1)
            k_aug, v_t, q_bias = _kv(x.reshape(BATCH, SEQ, D_MODEL), row(kv_norm),
                                     f32(w_kv), wf, fb,
                                     row(jnp.tile(fox_k_gain, FOX_HEADS)))
    return x.reshape(BATCH, SEQ, D_MODEL)
```

```python
import functools

import jax
import jax.numpy as jnp
from jax import lax
from jax.experimental import pallas as pl
from jax.experimental.pallas import tpu as pltpu

F32 = jnp.float32
BF16 = jnp.bfloat16

D_MODEL = 1024
BATCH = 8
SEQ = 2048
DEPTH = 4
N_MEM = 256
N_A_LAYERS = DEPTH // 2
MAIN_WIDTH = 768
MEM_WIDTH = 256
HG_HEAD_DIM = 128
HG_HEADS = MAIN_WIDTH // HG_HEAD_DIM
FOX_HEAD_DIM = 64
FOX_HEADS = MAIN_WIDTH // FOX_HEAD_DIM
MEM_HEADS = 4
MEM_HEAD_DIM = MEM_WIDTH // MEM_HEADS
D_FF = 2816
EPS = 1e-6
TOKENS = BATCH * SEQ

V7X_LANES = 128
V7X_VMEM_BYTES = 64 * 1024 * 1024

ROW_TILE = 512
PROJ_ROW_TILE = 1024
FF_CHUNK = 256
FFN_STAGE_SLOTS = 3
HG_CHUNK = 64
HG_SUB = 16
HG_SAFE_DECAY = 60.0
HG_GROUP = 4
HG_HEADS_PER_STEP = 2
FOX_TILE = 512
FOX_HEADS_PER_STEP = 2
ONES_ROWS = 16
MEM_BATCH = 2
KV_TILE = 1024
KV_CUM_SEG = 256
VMEM_LIMIT = 56 * 1024 * 1024


def _rms(x, gain):
    ms = jnp.mean(x * x, axis=-1, keepdims=True)
    return x * lax.rsqrt(ms + EPS) * gain


def _mm(a, w):
    return jnp.dot(a, w.astype(BF16), preferred_element_type=F32)


def _split_dot(x, rhs01, parts):
    acc = None
    rem = x
    for p in range(parts):
        piece = rem.astype(BF16)
        term = jnp.dot(piece, rhs01, preferred_element_type=F32)
        acc = term if acc is None else acc + term
        if p + 1 < parts:
            rem = rem - piece.astype(F32)
    return acc


def _group_ones(width, group):
    r = lax.broadcasted_iota(jnp.int32, (width, width), 0) // group
    c = lax.broadcasted_iota(jnp.int32, (width, width), 1) // group
    return (r == c).astype(BF16)


def _group_rms(x, gain_tiled, group):
    width = x.shape[-1]
    ss = _split_dot(x * x, _group_ones(width, group), 2)
    return x * lax.rsqrt(ss * (1.0 / group) + EPS) * gain_tiled


def _resident(shape, index=None):
    index = (0,) * len(shape) if index is None else index
    return pl.BlockSpec(shape, lambda *_: index, pipeline_mode=pl.Buffered(1))


def _layer(shape, layer, first=0):
    index = (layer, first) + (0,) * (len(shape) - 1)
    return pl.BlockSpec((None,) + tuple(shape), lambda *_: index,
                        pipeline_mode=pl.Buffered(1))


def _params(sem):
    return pltpu.CompilerParams(dimension_semantics=sem, vmem_limit_bytes=VMEM_LIMIT)


def _swiglu_weight_stream(layer, hbm, resident, stage_gu_ref, stage_d_ref, sem):
    wg_hbm, wu_hbm, wd_hbm = hbm
    wg_ref, wu_ref, wd_ref = resident

    def copies(c):
        slot = c % FFN_STAGE_SLOTS
        span = pl.ds(c * FF_CHUNK, FF_CHUNK)
        return (pltpu.make_async_copy(wg_hbm.at[layer, :, span], stage_gu_ref.at[0, slot],
                                      sem.at[0, slot]),
                pltpu.make_async_copy(wu_hbm.at[layer, :, span], stage_gu_ref.at[1, slot],
                                      sem.at[1, slot]),
                pltpu.make_async_copy(wd_hbm.at[layer, span, :], stage_d_ref.at[slot],
                                      sem.at[2, slot]))

    def start(c):
        for copy in copies(c):
            copy.start()

    def fetch(c):
        slot = c % FFN_STAGE_SLOTS
        sl = slice(c * FF_CHUNK, (c + 1) * FF_CHUNK)
        for copy in copies(c):
            copy.wait()
        wg_ref[:, sl] = stage_gu_ref[0, slot].astype(BF16)
        wu_ref[:, sl] = stage_gu_ref[1, slot].astype(BF16)
        wd_ref[sl, :] = stage_d_ref[slot].astype(BF16)
        if c + FFN_STAGE_SLOTS < D_FF // FF_CHUNK:
            start(c + FFN_STAGE_SLOTS)

    return start, fetch


def _swiglu_rows(x, gain_ref, resident, h_ref, fetch):
    wg_ref, wu_ref, wd_ref = resident
    n = _rms(x, gain_ref[...]).astype(BF16)
    for c in range(D_FF // FF_CHUNK):
        if fetch is not None:
            fetch(c)
        sl = slice(c * FF_CHUNK, (c + 1) * FF_CHUNK)
        g = jnp.dot(n, wg_ref[:, sl], preferred_element_type=F32)
        u = jnp.dot(n, wu_ref[:, sl], preferred_element_type=F32)
        h_ref[:, sl] = (g * jax.nn.sigmoid(g) * u).astype(BF16)
    y = jnp.dot(h_ref[...], wd_ref[...], preferred_element_type=F32)
    return x + 0.5 * y


def _swiglu_steps(layer, read_x, gain_ref, hbm, o_ref, scratch):
    h_ref, wg_ref, wu_ref, wd_ref, stage_gu_ref, stage_d_ref, sem = scratch
    resident = (wg_ref, wu_ref, wd_ref)
    start, fetch = _swiglu_weight_stream(layer, hbm, resident, stage_gu_ref, stage_d_ref, sem)

    @pl.when(pl.program_id(0) == 0)
    def _():
        for c in range(FFN_STAGE_SLOTS):
            start(c)
        o_ref[...] = _swiglu_rows(read_x(), gain_ref, resident, h_ref, fetch)

    @pl.when(pl.program_id(0) > 0)
    def _():
        o_ref[...] = _swiglu_rows(read_x(), gain_ref, resident, h_ref, None)


def _ffn_kernel(layer, x_ref, gain_ref, wg_hbm, wu_hbm, wd_hbm, o_ref, *scratch):
    _swiglu_steps(layer, lambda: x_ref[...], gain_ref, (wg_hbm, wu_hbm, wd_hbm), o_ref, scratch)


def _mix_ffn_kernel(layer, x_ref, main_ref, memo_ref, wo_main_ref, wo_mem_ref,
                    gain_ref, wg_hbm, wu_hbm, wd_hbm, o_ref, *scratch):
    def mixed_x():
        return (x_ref[...]
                + _mm(main_ref[...], wo_main_ref[...])
                + _mm(memo_ref[...], wo_mem_ref[...]))
    _swiglu_steps(layer, mixed_x, gain_ref, (wg_hbm, wu_hbm, wd_hbm), o_ref, scratch)


def _row_spec(width):
    return pl.BlockSpec((ROW_TILE, width), lambda i: (i, 0))


def _proj_spec(width):
    return pl.BlockSpec((PROJ_ROW_TILE, width), lambda i: (i, 0))


_IN_HBM = pl.BlockSpec(memory_space=pl.ANY)


def _swiglu_scratch(row_tile):
    return [
        pltpu.VMEM((row_tile, D_FF), BF16),
        pltpu.VMEM((D_MODEL, D_FF), BF16),
        pltpu.VMEM((D_MODEL, D_FF), BF16),
        pltpu.VMEM((D_FF, D_MODEL), BF16),
        pltpu.VMEM((2, FFN_STAGE_SLOTS, D_MODEL, FF_CHUNK), F32),
        pltpu.VMEM((FFN_STAGE_SLOTS, FF_CHUNK, D_MODEL), F32),
        pltpu.SemaphoreType.DMA((3, FFN_STAGE_SLOTS)),
    ]


def _ffn(layer, x, gains, wg, wu, wd):
    return pl.pallas_call(
        functools.partial(_ffn_kernel, layer),
        grid=(TOKENS // PROJ_ROW_TILE,),
        in_specs=[_proj_spec(D_MODEL), _layer((1, D_MODEL), layer), _IN_HBM, _IN_HBM, _IN_HBM],
        out_specs=_proj_spec(D_MODEL),
        out_shape=jax.ShapeDtypeStruct((TOKENS, D_MODEL), F32),
        scratch_shapes=_swiglu_scratch(PROJ_ROW_TILE),
        compiler_params=_params(("arbitrary",)),
        name="ffn",
    )(x, gains, wg, wu, wd)


def _mix_ffn(layer, x, main, memo, w_out, gains, wg, wu, wd):
    assert MAIN_WIDTH % MEM_WIDTH == 0
    return pl.pallas_call(
        functools.partial(_mix_ffn_kernel, layer),
        grid=(TOKENS // ROW_TILE,),
        in_specs=[_row_spec(D_MODEL), _row_spec(MAIN_WIDTH), _row_spec(MEM_WIDTH),
                  _layer((MAIN_WIDTH, D_MODEL), layer),
                  _layer((MEM_WIDTH, D_MODEL), layer, MAIN_WIDTH // MEM_WIDTH),
                  _layer((1, D_MODEL), layer), _IN_HBM, _IN_HBM, _IN_HBM],
        out_specs=_row_spec(D_MODEL),
        out_shape=jax.ShapeDtypeStruct((TOKENS, D_MODEL), F32),
        scratch_shapes=_swiglu_scratch(ROW_TILE),
        compiler_params=_params(("arbitrary",)),
        name="mix_ffn",
    )(x, main, memo, w_out, w_out, gains, wg, wu, wd)


def _inproj_a_kernel(layer, x_ref, gain_ref, w_ref, lbl_ref,
                     q_ref, lf_ref, i_ref, g_ref, qm_ref):
    n = _rms(x_ref[...], gain_ref[...]).astype(BF16)
    rows = [lbl_ref[r:r + 1, :] for r in range(N_A_LAYERS)]
    top = functools.reduce(jnp.maximum, rows)
    exps = [jnp.exp(r - top) for r in rows]
    total = functools.reduce(jnp.add, exps)
    lb = jnp.zeros_like(top)
    for r in range(1, layer + 1):
        lb = lb + exps[r] / total

    w = MAIN_WIDTH
    q_raw = _mm(n, w_ref[:, 0:w])
    q_ref[...] = q_raw * jax.nn.sigmoid(q_raw)
    f_raw = _mm(n, w_ref[:, w:2 * w])
    lf_ref[...] = jnp.log(lb + (1.0 - lb) * jax.nn.sigmoid(f_raw))
    i_ref[...] = _mm(n, w_ref[:, 2 * w:3 * w]).astype(BF16)
    g_raw = _mm(n, w_ref[:, 3 * w:4 * w])
    g_ref[...] = g_raw * jax.nn.sigmoid(g_raw)
    qm_ref[...] = _mm(n, w_ref[:, 4 * w:])


def _inproj_a(layer, x, gains, w_in, lb_logits):
    a_in = 4 * MAIN_WIDTH + MEM_WIDTH
    return pl.pallas_call(
        functools.partial(_inproj_a_kernel, layer),
        grid=(TOKENS // PROJ_ROW_TILE,),
        in_specs=[_proj_spec(D_MODEL), _layer((1, D_MODEL), layer),
                  _layer((D_MODEL, a_in), layer), _resident((N_A_LAYERS, MAIN_WIDTH))],
        out_specs=[_proj_spec(MAIN_WIDTH), _proj_spec(MAIN_WIDTH), _proj_spec(MAIN_WIDTH),
                   _proj_spec(MAIN_WIDTH), _proj_spec(MEM_WIDTH)],
        out_shape=[jax.ShapeDtypeStruct((TOKENS, MAIN_WIDTH), F32),
                   jax.ShapeDtypeStruct((TOKENS, MAIN_WIDTH), F32),
                   jax.ShapeDtypeStruct((TOKENS, MAIN_WIDTH), BF16),
                   jax.ShapeDtypeStruct((TOKENS, MAIN_WIDTH), F32),
                   jax.ShapeDtypeStruct((TOKENS, MEM_WIDTH), F32)],
        compiler_params=_params(("parallel",)),
        name="inproj_a",
    )(x, gains, w_in, lb_logits)


def _inproj_b_kernel(x_ref, gain_ref, w_ref, qg_ref, qbias_ref, qa_ref, gate_ref, qm_ref):
    n = _rms(x_ref[...], gain_ref[...]).astype(BF16)
    w = MAIN_WIDTH
    scale = FOX_HEAD_DIM ** -0.5 * LOG2E
    pair_w = 2 * FOX_HEAD_DIM
    q_raw = [_mm(n, w_ref[:, c * 256:(c + 1) * 256]) for c in range(w // 256)]
    gate = _mm(n, w_ref[:, w:2 * w])
    qm_ref[...] = _mm(n, w_ref[:, 2 * w:])
    for c in range(w // 256):
        sl = slice(c * 256, (c + 1) * 256)
        q_n = (_group_rms(q_raw[c], qg_ref[:, sl], FOX_HEAD_DIM) * scale).astype(BF16)
        for pp in range(256 // pair_w):
            lo = c * 256 + pp * pair_w
            for hh in range(2):
                dst = 2 * lo + hh * pair_w
                qa_ref[:, dst:dst + pair_w] = _with_bias_lanes(
                    q_n[:, pp * pair_w:(pp + 1) * pair_w], qbias_ref[:, lo:lo + pair_w], hh)
    gate_ref[...] = jax.nn.sigmoid(gate)


def _inproj_b(layer, x, gains, w_in, q_gain_tiled, q_bias):
    b_in = 2 * MAIN_WIDTH + MEM_WIDTH
    j = layer - N_A_LAYERS
    return pl.pallas_call(
        _inproj_b_kernel,
        grid=(TOKENS // PROJ_ROW_TILE,),
        in_specs=[_proj_spec(D_MODEL), _layer((1, D_MODEL), layer),
                  _layer((D_MODEL, b_in), j), _layer((1, MAIN_WIDTH), j),
                  _proj_spec(MAIN_WIDTH)],
        out_specs=[_proj_spec(2 * MAIN_WIDTH), _proj_spec(MAIN_WIDTH), _proj_spec(MEM_WIDTH)],
        out_shape=[jax.ShapeDtypeStruct((TOKENS, 2 * MAIN_WIDTH), BF16),
                   jax.ShapeDtypeStruct((TOKENS, MAIN_WIDTH), F32),
                   jax.ShapeDtypeStruct((TOKENS, MEM_WIDTH), F32)],
        compiler_params=_params(("parallel",)),
        name="inproj_b",
    )(x, gains, w_in, q_gain_tiled, q_bias)


def _hgrn_kernel(q_ref, lf_ref, v_ref, g_ref, gain_ref, o_ref,
                 cpad_ref, kpad_ref, vpad_ref):
    C, SB, HD = HG_CHUNK, HG_SUB, HG_HEAD_DIM
    n_sub = C // SB
    r_i = lax.broadcasted_iota(jnp.int32, (C, C), 0)
    c_i = lax.broadcasted_iota(jnp.int32, (C, C), 1)
    tril = (r_i >= c_i).astype(BF16)
    gain = gain_ref[...]

    def intra_exact(q, k, v16, c):
        outs = [jnp.zeros((SB, HD), F32)]
        for b in range(1, n_sub):
            lo = b * SB
            ref = c[lo - 1:lo, :]
            qb = (q[lo:lo + SB, :] * jnp.exp(c[lo:lo + SB, :] - ref)).astype(BF16)
            kb = (k[0:lo, :] * jnp.exp(ref - c[0:lo, :])).astype(BF16)
            p_b = lax.dot_general(qb, kb, (((1,), (1,)), ((), ())),
                                  preferred_element_type=F32)
            outs.append(jnp.dot(p_b.astype(BF16), v16[0:lo, :], preferred_element_type=F32))
        o = jnp.concatenate(outs, axis=0)
        ones = jnp.ones((HD, HD), BF16)
        row_in_block = lax.broadcasted_iota(jnp.int32, (C, HD), 0) % SB
        cpad_ref[SB:SB + C, :] = c
        kpad_ref[SB:SB + C, :] = k
        vpad_ref[SB:SB + C, :] = v16.astype(F32)
        for d in range(SB):
            cs = cpad_ref[SB - d:SB - d + C, :]
            ks = kpad_ref[SB - d:SB - d + C, :]
            vs = vpad_ref[SB - d:SB - d + C, :]
            wgt = jnp.where(row_in_block >= d, q * ks * jnp.exp(c - cs), 0.0)
            s = jnp.dot(wgt.astype(BF16), ones, preferred_element_type=F32)
            o = o + s * vs
        return o

    def chunk(n, st, intra, lanes):
        r0 = pl.multiple_of(n * C, C)
        lf = lf_ref[0, pl.ds(r0, C), lanes]
        q = q_ref[0, pl.ds(r0, C), lanes]
        v16 = v_ref[0, pl.ds(r0, C), lanes]
        k = 1.0 - jnp.exp(lf)
        c = None
        rem = lf
        for p in range(3):
            piece = rem.astype(BF16)
            term = jnp.dot(tril, piece, preferred_element_type=F32)
            c = term if c is None else c + term
            rem = rem - piece.astype(F32)
        c_end = c[C - 1:C, :]

        qe = (q * jnp.exp(c)).astype(BF16)
        o = lax.dot_general(qe, st.astype(BF16), (((1,), (1,)), ((), ())),
                            preferred_element_type=F32)
        kd = (k * jnp.exp(c_end - c)).astype(BF16)
        upd = lax.dot_general(v16, kd, (((0,), (0,)), ((), ())),
                              preferred_element_type=F32)
        o = o + intra(q, k, v16, c)
        o_ref[0, pl.ds(r0, C), lanes] = (
            _rms(o, gain) * g_ref[0, pl.ds(r0, C), lanes]).astype(BF16)
        return st * jnp.exp(c_end) + upd

    G = HG_GROUP
    R = G * C
    gr = lax.broadcasted_iota(jnp.int32, (R, R), 0)
    gc = lax.broadcasted_iota(jnp.int32, (R, R), 1)
    tril_group = ((gr // C == gc // C) & (gr >= gc)).astype(BF16)
    causal = r_i >= c_i

    def stage_cumsum(i):
        lf = lf_ref[0, i * R:(i + 1) * R, :]
        c = None
        rem = lf
        for p in range(3):
            piece = rem.astype(BF16)
            term = jnp.dot(tril_group, piece, preferred_element_type=F32)
            c = term if c is None else c + term
            rem = rem - piece.astype(F32)
        return lf, c

    def stage_scores(i, lanes, lf, c):
        lf, c = lf[:, lanes], c[:, lanes]
        q = q_ref[0, i * R:(i + 1) * R, lanes]
        v16 = v_ref[0, i * R:(i + 1) * R, lanes]
        k = 1.0 - jnp.exp(lf)
        lhs, rhs, c_ends = [], [], []
        for g in range(G):
            cg = c[g * C:(g + 1) * C, :]
            qg = q[g * C:(g + 1) * C, :]
            kg = k[g * C:(g + 1) * C, :]
            refs = [jnp.zeros((1, HD), F32)] + [cg[b * SB - 1:b * SB, :] for b in range(1, n_sub)]
            own_ref = jnp.concatenate([jnp.broadcast_to(r, (SB, HD)) for r in refs], axis=0)
            k_own = (kg * jnp.exp(own_ref - cg)).astype(BF16)
            lhs_cols, rhs_cols = [], []
            for b in range(n_sub):
                lo = b * SB
                q_b = (qg[lo:, :] * jnp.exp(cg[lo:, :] - refs[b])).astype(BF16)
                k_b = k_own[lo:lo + SB, :]
                above = [jnp.zeros((lo, HD), BF16)] if lo else []
                below = [jnp.zeros((C - lo - SB, HD), BF16)] if lo + SB < C else []
                lhs_cols.append(jnp.concatenate(above + [q_b], axis=0))
                rhs_cols.append(jnp.concatenate(above + [k_b] + below, axis=0))
            lhs.append(jnp.concatenate(lhs_cols, axis=1))
            rhs.append(jnp.concatenate(rhs_cols, axis=1))
            c_ends.append(cg[C - 1:C, :])
        c_end_rows = jnp.concatenate([jnp.broadcast_to(e, (C, HD)) for e in c_ends], axis=0)
        qe = (q * jnp.exp(c)).astype(BF16)
        kd = (k * jnp.exp(c_end_rows - c)).astype(BF16)

        scores = [lax.dot_general(lhs[g], rhs[g], (((1,), (1,)), ((), ())),
                                  preferred_element_type=F32) for g in range(G)]
        upds = [lax.dot_general(v16[g * C:(g + 1) * C, :], kd[g * C:(g + 1) * C, :],
                                (((0,), (0,)), ((), ())), preferred_element_type=F32)
                for g in range(G)]
        return scores, upds, qe, c_ends

    def stage_output(i, lanes, prepared, st):
        scores, upds, qe, c_ends = prepared
        v16 = v_ref[0, i * R:(i + 1) * R, lanes]
        intra = [jnp.dot(jnp.where(causal, scores[g], 0.0).astype(BF16),
                         v16[g * C:(g + 1) * C, :], preferred_element_type=F32)
                 for g in range(G)]
        outs = []
        for g in range(G):
            inter = lax.dot_general(qe[g * C:(g + 1) * C, :], st.astype(BF16),
                                    (((1,), (1,)), ((), ())), preferred_element_type=F32)
            outs.append(inter + intra[g])
            st = st * jnp.exp(c_ends[g]) + upds[g]
        o = jnp.concatenate(outs, axis=0)
        rows = slice(i * R, (i + 1) * R)
        o_ref[0, rows, lanes] = (_rms(o, gain) * g_ref[0, rows, lanes]).astype(BF16)
        return st

    head_lanes = [slice(hh * HD, (hh + 1) * HD) for hh in range(HG_HEADS_PER_STEP)]

    def run_factored():
        n_groups = SEQ // R
        first = stage_cumsum(0)
        prepared = [stage_scores(0, lanes, *first) for lanes in head_lanes]
        states = [jnp.zeros((HD, HD), F32) for _ in head_lanes]
        for i in range(n_groups):
            if i + 1 < n_groups:
                ahead = stage_cumsum(i + 1)
            states = [stage_output(i, lanes, prepared[hh], states[hh])
                      for hh, lanes in enumerate(head_lanes)]
            if i + 1 < n_groups:
                prepared = [stage_scores(i + 1, lanes, *ahead) for lanes in head_lanes]

    def run_exact():
        for lanes in head_lanes:
            def body(n, st, lanes=lanes):
                return chunk(n, st, intra_exact, lanes)
            lax.fori_loop(0, SEQ // C, body, jnp.zeros((HD, HD), F32))

    blk = lax.broadcasted_iota(jnp.int32, (SEQ // SB, SEQ), 0)
    tok = lax.broadcasted_iota(jnp.int32, (SEQ // SB, SEQ), 1) // SB
    block_decay = jnp.dot((blk == tok).astype(BF16), lf_ref[0].astype(BF16),
                          preferred_element_type=F32)
    mild = jnp.min(block_decay) >= -HG_SAFE_DECAY

    @pl.when(mild)
    def _():
        run_factored()

    @pl.when(jnp.logical_not(mild))
    def _():
        zpad = jnp.zeros((SB, HD), F32)
        cpad_ref[0:SB, :] = zpad
        kpad_ref[0:SB, :] = zpad
        vpad_ref[0:SB, :] = zpad
        run_exact()


def _hgrn(layer, q, lf, v, g, o_gains):
    spec = pl.BlockSpec((1, SEQ, HG_HEADS_PER_STEP * HG_HEAD_DIM), lambda b, h: (b, 0, h))
    return pl.pallas_call(
        _hgrn_kernel,
        grid=(BATCH, HG_HEADS // HG_HEADS_PER_STEP),
        in_specs=[spec, spec, spec, spec, _layer((1, HG_HEAD_DIM), layer)],
        out_specs=spec,
        out_shape=jax.ShapeDtypeStruct((BATCH, SEQ, MAIN_WIDTH), BF16),
        scratch_shapes=[pltpu.VMEM((HG_SUB + HG_CHUNK, HG_HEAD_DIM), F32),
                        pltpu.VMEM((HG_SUB + HG_CHUNK, HG_HEAD_DIM), F32),
                        pltpu.VMEM((HG_SUB + HG_CHUNK, HG_HEAD_DIM), F32)],
        compiler_params=_params(("parallel", "parallel")),
        name="hgrn",
    )(q, lf, v, g, o_gains)


FOX_BIAS_PIECES = 3
FOX_PIECE_STRIDE = 16
LOG2E = 1.4426950408889634


def _bias_slot_layout():
    lane = lax.broadcasted_iota(jnp.int32, (V7X_LANES, MAIN_WIDTH), 1)
    pair = lane // (2 * FOX_HEAD_DIM)
    half = (lane // FOX_HEAD_DIM) % 2
    head_here = 2 * pair + (1 - half)
    slot = lane % FOX_HEAD_DIM
    return head_here, slot


def _place_bias(packed):
    head_here, slot = _bias_slot_layout()
    src = lax.broadcasted_iota(jnp.int32, (V7X_LANES, MAIN_WIDTH), 0)
    src_piece, src_head = src // FOX_PIECE_STRIDE, src % FOX_PIECE_STRIDE
    hit = (src_piece < FOX_BIAS_PIECES) & (src_head == head_here)
    place = jnp.concatenate([(hit & (slot == src_piece)).astype(BF16),
                             (hit & (slot == src_piece + FOX_BIAS_PIECES)).astype(BF16)],
                            axis=1)
    return jnp.dot(packed, place, preferred_element_type=F32)


def _with_bias_lanes(pair_vals, pair_bias, head_in_pair):
    own = (lax.broadcasted_iota(jnp.int32, pair_vals.shape, 1) // FOX_HEAD_DIM) == head_in_pair
    return jnp.where(own, pair_vals, pair_bias)


def _kv_kernel(x_ref, gain_ref, wk_ref, wv_ref, wf_ref, fb_ref, kg_ref,
               ka_ref, vt_ref, qbias_ref, carry_ref, v_ref):
    @pl.when(pl.program_id(1) == 0)
    def _():
        carry_ref[...] = jnp.zeros_like(carry_ref)

    n = _rms(x_ref[0], gain_ref[...]).astype(BF16)

    z = _mm(n, wf_ref[...]) + fb_ref[...]
    v_ref[...] = _mm(n, wv_ref[...])
    k_raw = [_mm(n, wk_ref[:, c * 256:(c + 1) * 256]) for c in range(MAIN_WIDTH // 256)]
    log_f = jnp.minimum(z, 0.0) - jnp.log1p(jnp.exp(-jnp.abs(z)))
    seg = KV_CUM_SEG
    tril = (lax.broadcasted_iota(jnp.int32, (seg, seg), 0)
            >= lax.broadcasted_iota(jnp.int32, (seg, seg), 1)).astype(BF16)
    local = []
    for sgi in range(KV_TILE // seg):
        rem = log_f[sgi * seg:(sgi + 1) * seg, :]
        acc = None
        for p in range(3):
            piece = rem.astype(BF16)
            term = jnp.dot(tril, piece, preferred_element_type=F32)
            acc = term if acc is None else acc + term
            rem = rem - piece.astype(F32)
        local.append(acc)
    last = carry_ref[...]
    parts = []
    for acc in local:
        parts.append(acc + last)
        last = parts[-1][seg - 1:seg, :]
    carry_ref[...] = last
    cum = jnp.concatenate(parts, axis=0)

    lane = lax.broadcasted_iota(jnp.int32, cum.shape, 1)
    rem = jnp.where(lane < FOX_HEADS, cum * LOG2E, 0.0)
    packed = None
    for p in range(FOX_BIAS_PIECES):
        piece = rem.astype(BF16).astype(F32)
        moved = pltpu.roll(piece, p * FOX_PIECE_STRIDE, axis=1) if p else piece
        packed = moved if packed is None else packed + moved
        rem = rem - piece
    placed = _place_bias(packed.astype(BF16))
    _, slot = _bias_slot_layout()
    slot_row = slot[0:1, :]
    n_p = FOX_BIAS_PIECES
    k_bias = (placed[:, :MAIN_WIDTH]
              + ((slot_row >= n_p) & (slot_row < 2 * n_p)).astype(F32)).astype(BF16)
    qbias_ref[0] = (placed[:, MAIN_WIDTH:] - (slot_row < n_p).astype(F32)).astype(BF16)

    pair_w = 2 * FOX_HEAD_DIM
    for c in range(MAIN_WIDTH // 256):
        sl = slice(c * 256, (c + 1) * 256)
        k_n = _group_rms(k_raw[c], kg_ref[:, sl], FOX_HEAD_DIM).astype(BF16)
        for pp in range(256 // pair_w):
            lo = c * 256 + pp * pair_w
            for hh in range(2):
                dst = 2 * lo + hh * pair_w
                ka_ref[0, :, dst:dst + pair_w] = _with_bias_lanes(
                    k_n[:, pp * pair_w:(pp + 1) * pair_w], k_bias[:, lo:lo + pair_w], hh)
    vt_ref[0] = v_ref[...].T.astype(BF16)


def _kv(x, gain, w_kv, wf, f_bias, k_gain_tiled):
    row = lambda w: pl.BlockSpec((1, KV_TILE, w), lambda b, i: (b, i, 0))
    return pl.pallas_call(
        _kv_kernel,
        grid=(BATCH, SEQ // KV_TILE),
        in_specs=[row(D_MODEL), _resident((1, D_MODEL)),
                  _resident((D_MODEL, MAIN_WIDTH), (0, 0)),
                  _resident((D_MODEL, MAIN_WIDTH), (0, 1)), _resident((D_MODEL, V7X_LANES)),
                  _resident((1, V7X_LANES)), _resident((1, MAIN_WIDTH))],
        out_specs=[row(2 * MAIN_WIDTH),
                   pl.BlockSpec((1, MAIN_WIDTH, KV_TILE), lambda b, i: (b, 0, i)),
                   row(MAIN_WIDTH)],
        out_shape=[jax.ShapeDtypeStruct((BATCH, SEQ, 2 * MAIN_WIDTH), BF16),
                   jax.ShapeDtypeStruct((BATCH, MAIN_WIDTH, SEQ), BF16),
                   jax.ShapeDtypeStruct((BATCH, SEQ, MAIN_WIDTH), BF16)],
        scratch_shapes=[pltpu.VMEM((1, V7X_LANES), F32),
                        pltpu.VMEM((KV_TILE, MAIN_WIDTH), F32)],
        compiler_params=_params(("parallel", "arbitrary")),
        name="kv",
    )(x, gain, w_kv, w_kv, wf, f_bias, k_gain_tiled)


def _fox_kernel(qa_ref, ka_ref, vt_ref, gate_ref, o_ref, s_ref, m_ref, acc_ref):
    T, HD = FOX_TILE, FOX_HEAD_DIM
    W = 2 * HD
    H = T // 2
    NH = FOX_HEADS_PER_STEP

    def pieces(i, j):
        if j == i:
            return [(slice(0, H), slice(0, T)), (slice(H, T), slice(H, T))]
        return [(slice(0, T), slice(0, T))]

    def issue_scores(slot, i, j):
        for hh in range(NH):
            for keys, cols in pieces(i, j):
                s_ref[slot, hh, keys, cols] = lax.dot_general(
                    ka_ref[0, j * T + keys.start:j * T + keys.stop, hh * W:(hh + 1) * W],
                    qa_ref[0, i * T + cols.start:i * T + cols.stop, hh * W:(hh + 1) * W],
                    (((1,), (1,)), ((), ())), preferred_element_type=F32)

    def consume(slot, i, j):
        for n_piece, (keys, cols) in enumerate(pieces(i, j)):
            n_k, n_q = keys.stop - keys.start, cols.stop - cols.start
            first = j == 0 and n_piece == 0
            probs, alphas = [], []
            for hh in range(NH):
                s = s_ref[slot, hh, keys, cols]
                if j == i:
                    key_pos = lax.broadcasted_iota(jnp.int32, (n_k, n_q), 0) + keys.start
                    query_pos = lax.broadcasted_iota(jnp.int32, (n_k, n_q), 1) + cols.start
                    s = jnp.where(key_pos <= query_pos, s, -jnp.inf)
                m_new = jnp.max(s, axis=0, keepdims=True)
                if not first:
                    m_old = m_ref[hh, :, cols]
                    m_new = jnp.maximum(m_old, m_new)
                    alphas.append(jnp.exp2(m_old - m_new))
                probs.append(jnp.exp2(s - m_new).astype(BF16))
                m_ref[hh, :, cols] = m_new
            for hh in range(NH):
                vt = jnp.concatenate(
                    [vt_ref[0, hh * HD:(hh + 1) * HD, j * T + keys.start:j * T + keys.stop],
                     jnp.ones((ONES_ROWS, n_k), BF16)], axis=0)
                pv = jnp.dot(vt, probs[hh], preferred_element_type=F32)
                acc_ref[hh, :, cols] = pv if first else alphas[hh] * acc_ref[hh, :, cols] + pv
        if j == i:
            out_t = jnp.concatenate([acc_ref[hh, 0:HD, :] / acc_ref[hh, HD:HD + 1, :]
                                     for hh in range(NH)], axis=0)
            rows = slice(i * T, (i + 1) * T)
            o_ref[0, rows, :] = (out_t.T * gate_ref[0, rows, :]).astype(BF16)

    blocks = [(i, j) for i in range(SEQ // T) for j in range(i + 1)]
    issue_scores(0, *blocks[0])
    for n, (i, j) in enumerate(blocks):
        if n + 1 < len(blocks):
            issue_scores((n + 1) % 2, *blocks[n + 1])
        consume(n % 2, i, j)


def _fox(qa, ka, vt, gate):
    T = FOX_TILE
    W = 2 * FOX_HEAD_DIM
    NH = FOX_HEADS_PER_STEP
    seq_spec = lambda w: pl.BlockSpec((1, SEQ, w), lambda b, p: (b, 0, p))
    return pl.pallas_call(
        _fox_kernel,
        grid=(BATCH, FOX_HEADS // NH),
        in_specs=[seq_spec(NH * W), seq_spec(NH * W),
                  pl.BlockSpec((1, NH * FOX_HEAD_DIM, SEQ), lambda b, p: (b, p, 0)),
                  seq_spec(NH * FOX_HEAD_DIM)],
        out_specs=seq_spec(NH * FOX_HEAD_DIM),
        out_shape=jax.ShapeDtypeStruct((BATCH, SEQ, MAIN_WIDTH), BF16),
        scratch_shapes=[pltpu.VMEM((2, NH, T, T), F32),
                        pltpu.VMEM((NH, 1, T), F32),
                        pltpu.VMEM((NH, FOX_HEAD_DIM + ONES_ROWS, T), F32)],
        compiler_params=_params(("parallel", "parallel")),
        name="fox",
    )(qa, ka, vt, gate)


def _memattn_kernel(qm_ref, mem_ref, mgain_ref, wkv_ref, qg_ref, kg_ref, o_ref, kv_ref):
    W, HD = MEM_WIDTH, MEM_HEAD_DIM
    batch = range(MEM_BATCH)
    heads = range(MEM_HEADS)
    ones_rows = jnp.ones((ONES_ROWS, N_MEM), BF16)
    scale = HD ** -0.5 * LOG2E

    for bi in batch:
        mem_n = _rms(mem_ref[bi], mgain_ref[...]).astype(BF16)
        kv_ref[bi] = _mm(mem_n, wkv_ref[...])
    keys, values_t = [], []
    for bi in batch:
        km = _group_rms(kv_ref[bi, :, :W], kg_ref[...], HD).astype(BF16)
        head_of_lane = lax.broadcasted_iota(jnp.int32, km.shape, 1) // HD
        vm_t = kv_ref[bi, :, W:].T.astype(BF16)
        keys.append([jnp.where(head_of_lane == h, km, jnp.zeros_like(km)) for h in heads])
        values_t.append([jnp.concatenate([vm_t[h * HD:(h + 1) * HD, :], ones_rows], axis=0)
                         for h in heads])
    queries = [(_group_rms(qm_ref[bi], qg_ref[...], HD) * scale).astype(BF16) for bi in batch]
    scores = [[lax.dot_general(keys[bi][h], queries[bi], (((1,), (1,)), ((), ())),
                               preferred_element_type=F32) for h in heads] for bi in batch]
    for bi in batch:
        outs = []
        for h in heads:
            s = scores[bi][h]
            p = jnp.exp2(s - jnp.max(s, axis=0, keepdims=True)).astype(BF16)
            r = jnp.dot(values_t[bi][h], p, preferred_element_type=F32)
            outs.append(r[0:HD, :] / r[HD:HD + 1, :])
        o_ref[bi] = jnp.concatenate(outs, axis=0).T.astype(BF16)


def _memattn(layer, qm, mem, mem_gains, w_mem_kv, q_gains_tiled, k_gains_tiled):
    qspec = pl.BlockSpec((MEM_BATCH, SEQ, MEM_WIDTH), lambda b: (b, 0, 0))
    return pl.pallas_call(
        _memattn_kernel,
        grid=(BATCH // MEM_BATCH,),
        in_specs=[qspec,
                  pl.BlockSpec((MEM_BATCH, N_MEM, D_MODEL), lambda b: (b, 0, 0)),
                  _layer((1, D_MODEL), layer), _layer((D_MODEL, 2 * MEM_WIDTH), layer),
                  _layer((1, MEM_WIDTH), layer), _layer((1, MEM_WIDTH), layer)],
        out_specs=qspec,
        out_shape=jax.ShapeDtypeStruct((BATCH, SEQ, MEM_WIDTH), BF16),
        scratch_shapes=[pltpu.VMEM((MEM_BATCH, N_MEM, 2 * MEM_WIDTH), F32)],
        compiler_params=_params(("parallel",)),
        name="memattn",
    )(qm, mem, mem_gains, w_mem_kv, q_gains_tiled, k_gains_tiled)


def kernel(x, mem, ffn1_norm, ffn1_w_gate, ffn1_w_up, ffn1_w_down, mix_norm, mem_norm,
           w_mem_kv, mem_q_gain, mem_k_gain, w_in_a, hgrn_lb_logits, hgrn_o_gain,
           w_in_b, fox_q_gain, kv_norm, w_kv, fox_f_bias, fox_k_gain, w_out,
           ffn2_norm, ffn2_w_gate, ffn2_w_up, ffn2_w_down):
    row = lambda t: t.reshape(1, -1).astype(F32)
    rows = lambda t, reps=1: jnp.tile(t.astype(F32), (1, reps)).reshape(t.shape[0], 1, -1)
    f32 = lambda t: t.astype(F32)
    x = x.reshape(TOKENS, D_MODEL).astype(F32)
    mem = mem.astype(F32)
    lb_logits = hgrn_lb_logits.astype(F32)
    ffn1 = (rows(ffn1_norm), f32(ffn1_w_gate), f32(ffn1_w_up), f32(ffn1_w_down))
    ffn2 = (rows(ffn2_norm), f32(ffn2_w_gate), f32(ffn2_w_up), f32(ffn2_w_down))
    mix_gains = rows(mix_norm)
    o_gains, fox_q_gains = rows(hgrn_o_gain), rows(fox_q_gain, FOX_HEADS)
    mem_args = (rows(mem_norm), f32(w_mem_kv), rows(mem_q_gain, MEM_HEADS),
                rows(mem_k_gain, MEM_HEADS))

    for l in range(DEPTH):
        x = _ffn(l, x, *ffn1)
        if l < N_A_LAYERS:
            qs, lf, iv, sg, qm = _inproj_a(l, x, mix_gains, f32(w_in_a), lb_logits)
            to3 = lambda t: t.reshape(BATCH, SEQ, MAIN_WIDTH)
            main = _hgrn(l, to3(qs), to3(lf), to3(iv), to3(sg), o_gains)
        else:
            qa, gate, qm = _inproj_b(l, x, mix_gains, f32(w_in_b), fox_q_gains,
                                     q_bias.reshape(TOKENS, MAIN_WIDTH))
            main = _fox(qa.reshape(BATCH, SEQ, 2 * MAIN_WIDTH), k_aug, v_t,
                        gate.reshape(BATCH, SEQ, MAIN_WIDTH))
        memo = _memattn(l, qm.reshape(BATCH, SEQ, MEM_WIDTH), mem, *mem_args)
        x = _mix_ffn(l, x, main.reshape(TOKENS, MAIN_WIDTH), memo.reshape(TOKENS, MEM_WIDTH),
                     f32(w_out), *ffn2)
        if l == N_A_LAYERS - 1:
            pad = V7X_LANES - FOX_HEADS
            wf = jnp.pad(f32(w_kv[:, 2 * MAIN_WIDTH:]), ((0, 0), (0, pad)))
            fb = jnp.pad(fox_f_bias.astype(F32), (0, pad)).reshape(1, -1)
            k_aug, v_t, q_bias = _kv(x.reshape(BATCH, SEQ, D_MODEL), row(kv_norm),
                                     f32(w_kv), wf, fb,
                                     row(jnp.tile(fox_k_gain, FOX_HEADS)))
    return x.reshape(BATCH, SEQ, D_MODEL)
```

```python
import functools

import jax
import jax.numpy as jnp
from jax import lax
from jax.experimental import pallas as pl
from jax.experimental.pallas import tpu as pltpu

F32 = jnp.float32
BF16 = jnp.bfloat16

D_MODEL = 1024
BATCH = 8
SEQ = 2048
DEPTH = 4
N_MEM = 256
N_A_LAYERS = DEPTH // 2
MAIN_WIDTH = 768
MEM_WIDTH = 256
HG_HEAD_DIM = 128
HG_HEADS = MAIN_WIDTH // HG_HEAD_DIM
FOX_HEAD_DIM = 64
FOX_HEADS = MAIN_WIDTH // FOX_HEAD_DIM
MEM_HEADS = 4
MEM_HEAD_DIM = MEM_WIDTH // MEM_HEADS
D_FF = 2816
EPS = 1e-6
TOKENS = BATCH * SEQ

V7X_LANES = 128
V7X_MXU_COLS = 256
V7X_VMEM_BYTES = 64 * 1024 * 1024

ROW_TILE = 512
PROJ_ROW_TILE = 1024
FF_CHUNK = 256
FFN_STAGE_SLOTS = 3
HG_CHUNK = 64
HG_SUB = 16
HG_SAFE_DECAY = 60.0
HG_GROUP = 4
HG_HEADS_PER_STEP = 2
FOX_TILE = 512
FOX_HEADS_PER_STEP = 2
ONES_ROWS = 16
MEM_BATCH = 2
KV_TILE = 1024
KV_CUM_SEG = 256
VMEM_LIMIT = V7X_VMEM_BYTES * 7 // 8


def _rms(x, gain):
    ms = jnp.mean(x * x, axis=-1, keepdims=True)
    return x * lax.rsqrt(ms + EPS) * gain


def _mm(a, w):
    return jnp.dot(a, w.astype(BF16), preferred_element_type=F32)


def _split_dot(x, rhs01, parts):
    acc = None
    rem = x
    for p in range(parts):
        piece = rem.astype(BF16)
        term = jnp.dot(piece, rhs01, preferred_element_type=F32)
        acc = term if acc is None else acc + term
        if p + 1 < parts:
            rem = rem - piece.astype(F32)
    return acc


def _group_ones(width, group):
    r = lax.broadcasted_iota(jnp.int32, (width, width), 0) // group
    c = lax.broadcasted_iota(jnp.int32, (width, width), 1) // group
    return (r == c).astype(BF16)


def _group_rms(x, gain_tiled, group):
    width = x.shape[-1]
    ss = _split_dot(x * x, _group_ones(width, group), 2)
    return x * lax.rsqrt(ss * (1.0 / group) + EPS) * gain_tiled


def _resident(shape, index=None):
    index = (0,) * len(shape) if index is None else index
    return pl.BlockSpec(shape, lambda *_: index, pipeline_mode=pl.Buffered(1))


def _layer(shape, layer, first=0):
    index = (layer, first) + (0,) * (len(shape) - 1)
    return pl.BlockSpec((None,) + tuple(shape), lambda *_: index,
                        pipeline_mode=pl.Buffered(1))


def _params(sem):
    return pltpu.CompilerParams(dimension_semantics=sem, vmem_limit_bytes=VMEM_LIMIT)


def _swiglu_weight_stream(layer, hbm, resident, stage_gu_ref, stage_d_ref, sem):
    wg_hbm, wu_hbm, wd_hbm = hbm
    wg_ref, wu_ref, wd_ref = resident

    def copies(c):
        slot = c % FFN_STAGE_SLOTS
        span = pl.ds(c * FF_CHUNK, FF_CHUNK)
        return (pltpu.make_async_copy(wg_hbm.at[layer, :, span], stage_gu_ref.at[0, slot],
                                      sem.at[0, slot]),
                pltpu.make_async_copy(wu_hbm.at[layer, :, span], stage_gu_ref.at[1, slot],
                                      sem.at[1, slot]),
                pltpu.make_async_copy(wd_hbm.at[layer, span, :], stage_d_ref.at[slot],
                                      sem.at[2, slot]))

    def start(c):
        for copy in copies(c):
            copy.start()

    def fetch(c):
        slot = c % FFN_STAGE_SLOTS
        sl = slice(c * FF_CHUNK, (c + 1) * FF_CHUNK)
        for copy in copies(c):
            copy.wait()
        wg_ref[:, sl] = stage_gu_ref[0, slot].astype(BF16)
        wu_ref[:, sl] = stage_gu_ref[1, slot].astype(BF16)
        wd_ref[sl, :] = stage_d_ref[slot].astype(BF16)
        if c + FFN_STAGE_SLOTS < D_FF // FF_CHUNK:
            start(c + FFN_STAGE_SLOTS)

    return start, fetch


def _swiglu_rows(x, gain_ref, resident, h_ref, fetch):
    wg_ref, wu_ref, wd_ref = resident
    n = _rms(x, gain_ref[...]).astype(BF16)
    for c in range(D_FF // FF_CHUNK):
        if fetch is not None:
            fetch(c)
        sl = slice(c * FF_CHUNK, (c + 1) * FF_CHUNK)
        g = jnp.dot(n, wg_ref[:, sl], preferred_element_type=F32)
        u = jnp.dot(n, wu_ref[:, sl], preferred_element_type=F32)
        h_ref[:, sl] = (g * jax.nn.sigmoid(g) * u).astype(BF16)
    y = jnp.dot(h_ref[...], wd_ref[...], preferred_element_type=F32)
    return x + 0.5 * y


def _swiglu_steps(layer, read_x, gain_ref, hbm, o_ref, scratch):
    h_ref, wg_ref, wu_ref, wd_ref, stage_gu_ref, stage_d_ref, sem = scratch
    resident = (wg_ref, wu_ref, wd_ref)
    start, fetch = _swiglu_weight_stream(layer, hbm, resident, stage_gu_ref, stage_d_ref, sem)

    @pl.when(pl.program_id(0) == 0)
    def _():
        for c in range(FFN_STAGE_SLOTS):
            start(c)
        o_ref[...] = _swiglu_rows(read_x(), gain_ref, resident, h_ref, fetch)

    @pl.when(pl.program_id(0) > 0)
    def _():
        o_ref[...] = _swiglu_rows(read_x(), gain_ref, resident, h_ref, None)


def _ffn_kernel(layer, x_ref, gain_ref, wg_hbm, wu_hbm, wd_hbm, o_ref, *scratch):
    _swiglu_steps(layer, lambda: x_ref[...], gain_ref, (wg_hbm, wu_hbm, wd_hbm), o_ref, scratch)


def _mix_ffn_kernel(layer, x_ref, main_ref, memo_ref, wo_main_ref, wo_mem_ref,
                    gain_ref, wg_hbm, wu_hbm, wd_hbm, o_ref, *scratch):
    def mixed_x():
        return (x_ref[...]
                + _mm(main_ref[...], wo_main_ref[...])
                + _mm(memo_ref[...], wo_mem_ref[...]))
    _swiglu_steps(layer, mixed_x, gain_ref, (wg_hbm, wu_hbm, wd_hbm), o_ref, scratch)


def _row_spec(width):
    return pl.BlockSpec((ROW_TILE, width), lambda i: (i, 0))


def _proj_spec(width):
    return pl.BlockSpec((PROJ_ROW_TILE, width), lambda i: (i, 0))


_IN_HBM = pl.BlockSpec(memory_space=pl.ANY)
_SWIGLU_SCRATCH = [
    pltpu.VMEM((ROW_TILE, D_FF), BF16),
    pltpu.VMEM((D_MODEL, D_FF), BF16),
    pltpu.VMEM((D_MODEL, D_FF), BF16),
    pltpu.VMEM((D_FF, D_MODEL), BF16),
    pltpu.VMEM((2, FFN_STAGE_SLOTS, D_MODEL, FF_CHUNK), F32),
    pltpu.VMEM((FFN_STAGE_SLOTS, FF_CHUNK, D_MODEL), F32),
    pltpu.SemaphoreType.DMA((3, FFN_STAGE_SLOTS)),
]


def _ffn(layer, x, gains, wg, wu, wd):
    return pl.pallas_call(
        functools.partial(_ffn_kernel, layer),
        grid=(TOKENS // ROW_TILE,),
        in_specs=[_row_spec(D_MODEL), _layer((1, D_MODEL), layer), _IN_HBM, _IN_HBM, _IN_HBM],
        out_specs=_row_spec(D_MODEL),
        out_shape=jax.ShapeDtypeStruct((TOKENS, D_MODEL), F32),
        scratch_shapes=_SWIGLU_SCRATCH,
        compiler_params=_params(("arbitrary",)),
        name="ffn",
    )(x, gains, wg, wu, wd)


def _mix_ffn(layer, x, main, memo, w_out, gains, wg, wu, wd):
    assert MAIN_WIDTH % MEM_WIDTH == 0
    return pl.pallas_call(
        functools.partial(_mix_ffn_kernel, layer),
        grid=(TOKENS // ROW_TILE,),
        in_specs=[_row_spec(D_MODEL), _row_spec(MAIN_WIDTH), _row_spec(MEM_WIDTH),
                  _layer((MAIN_WIDTH, D_MODEL), layer),
                  _layer((MEM_WIDTH, D_MODEL), layer, MAIN_WIDTH // MEM_WIDTH),
                  _layer((1, D_MODEL), layer), _IN_HBM, _IN_HBM, _IN_HBM],
        out_specs=_row_spec(D_MODEL),
        out_shape=jax.ShapeDtypeStruct((TOKENS, D_MODEL), F32),
        scratch_shapes=_SWIGLU_SCRATCH,
        compiler_params=_params(("arbitrary",)),
        name="mix_ffn",
    )(x, main, memo, w_out, w_out, gains, wg, wu, wd)


def _inproj_a_kernel(layer, x_ref, gain_ref, w_ref, lbl_ref,
                     q_ref, lf_ref, i_ref, g_ref, qm_ref):
    n = _rms(x_ref[...], gain_ref[...]).astype(BF16)
    rows = [lbl_ref[r:r + 1, :] for r in range(N_A_LAYERS)]
    top = functools.reduce(jnp.maximum, rows)
    exps = [jnp.exp(r - top) for r in rows]
    total = functools.reduce(jnp.add, exps)
    lb = jnp.zeros_like(top)
    for r in range(1, layer + 1):
        lb = lb + exps[r] / total

    w = MAIN_WIDTH
    q_raw = _mm(n, w_ref[:, 0:w])
    q_ref[...] = q_raw * jax.nn.sigmoid(q_raw)
    f_raw = _mm(n, w_ref[:, w:2 * w])
    lf_ref[...] = jnp.log(lb + (1.0 - lb) * jax.nn.sigmoid(f_raw))
    i_ref[...] = _mm(n, w_ref[:, 2 * w:3 * w]).astype(BF16)
    g_raw = _mm(n, w_ref[:, 3 * w:4 * w])
    g_ref[...] = g_raw * jax.nn.sigmoid(g_raw)
    qm_ref[...] = _mm(n, w_ref[:, 4 * w:])


def _inproj_a(layer, x, gains, w_in, lb_logits):
    a_in = 4 * MAIN_WIDTH + MEM_WIDTH
    return pl.pallas_call(
        functools.partial(_inproj_a_kernel, layer),
        grid=(TOKENS // PROJ_ROW_TILE,),
        in_specs=[_proj_spec(D_MODEL), _layer((1, D_MODEL), layer),
                  _layer((D_MODEL, a_in), layer), _resident((N_A_LAYERS, MAIN_WIDTH))],
        out_specs=[_proj_spec(MAIN_WIDTH), _proj_spec(MAIN_WIDTH), _proj_spec(MAIN_WIDTH),
                   _proj_spec(MAIN_WIDTH), _proj_spec(MEM_WIDTH)],
        out_shape=[jax.ShapeDtypeStruct((TOKENS, MAIN_WIDTH), F32),
                   jax.ShapeDtypeStruct((TOKENS, MAIN_WIDTH), F32),
                   jax.ShapeDtypeStruct((TOKENS, MAIN_WIDTH), BF16),
                   jax.ShapeDtypeStruct((TOKENS, MAIN_WIDTH), F32),
                   jax.ShapeDtypeStruct((TOKENS, MEM_WIDTH), F32)],
        compiler_params=_params(("parallel",)),
        name="inproj_a",
    )(x, gains, w_in, lb_logits)


def _inproj_b_kernel(x_ref, gain_ref, w_ref, qg_ref, qbias_ref, qa_ref, gate_ref, qm_ref):
    n = _rms(x_ref[...], gain_ref[...]).astype(BF16)
    w = MAIN_WIDTH
    scale = FOX_HEAD_DIM ** -0.5 * LOG2E
    pair_w = 2 * FOX_HEAD_DIM
    q_raw = [_mm(n, w_ref[:, c * V7X_MXU_COLS:(c + 1) * V7X_MXU_COLS])
             for c in range(w // V7X_MXU_COLS)]
    gate = _mm(n, w_ref[:, w:2 * w])
    qm_ref[...] = _mm(n, w_ref[:, 2 * w:])
    for c in range(w // V7X_MXU_COLS):
        sl = slice(c * V7X_MXU_COLS, (c + 1) * V7X_MXU_COLS)
        q_n = (_group_rms(q_raw[c], qg_ref[:, sl], FOX_HEAD_DIM) * scale).astype(BF16)
        for pp in range(V7X_MXU_COLS // pair_w):
            lo = c * V7X_MXU_COLS + pp * pair_w
            for hh in range(2):
                dst = 2 * lo + hh * pair_w
                qa_ref[:, dst:dst + pair_w] = _with_bias_lanes(
                    q_n[:, pp * pair_w:(pp + 1) * pair_w], qbias_ref[:, lo:lo + pair_w], hh)
    gate_ref[...] = jax.nn.sigmoid(gate)


def _inproj_b(layer, x, gains, w_in, q_gain_tiled, q_bias):
    b_in = 2 * MAIN_WIDTH + MEM_WIDTH
    j = layer - N_A_LAYERS
    return pl.pallas_call(
        _inproj_b_kernel,
        grid=(TOKENS // PROJ_ROW_TILE,),
        in_specs=[_proj_spec(D_MODEL), _layer((1, D_MODEL), layer),
                  _layer((D_MODEL, b_in), j), _layer((1, MAIN_WIDTH), j),
                  _proj_spec(MAIN_WIDTH)],
        out_specs=[_proj_spec(2 * MAIN_WIDTH), _proj_spec(MAIN_WIDTH), _proj_spec(MEM_WIDTH)],
        out_shape=[jax.ShapeDtypeStruct((TOKENS, 2 * MAIN_WIDTH), BF16),
                   jax.ShapeDtypeStruct((TOKENS, MAIN_WIDTH), F32),
                   jax.ShapeDtypeStruct((TOKENS, MEM_WIDTH), F32)],
        compiler_params=_params(("parallel",)),
        name="inproj_b",
    )(x, gains, w_in, q_gain_tiled, q_bias)


def _hgrn_kernel(q_ref, lf_ref, v_ref, g_ref, gain_ref, o_ref,
                 cpad_ref, kpad_ref, vpad_ref):
    C, SB, HD = HG_CHUNK, HG_SUB, HG_HEAD_DIM
    n_sub = C // SB
    r_i = lax.broadcasted_iota(jnp.int32, (C, C), 0)
    c_i = lax.broadcasted_iota(jnp.int32, (C, C), 1)
    tril = (r_i >= c_i).astype(BF16)
    gain = gain_ref[...]

    def intra_exact(q, k, v16, c):
        outs = [jnp.zeros((SB, HD), F32)]
        for b in range(1, n_sub):
            lo = b * SB
            ref = c[lo - 1:lo, :]
            qb = (q[lo:lo + SB, :] * jnp.exp(c[lo:lo + SB, :] - ref)).astype(BF16)
            kb = (k[0:lo, :] * jnp.exp(ref - c[0:lo, :])).astype(BF16)
            p_b = lax.dot_general(qb, kb, (((1,), (1,)), ((), ())),
                                  preferred_element_type=F32)
            outs.append(jnp.dot(p_b.astype(BF16), v16[0:lo, :], preferred_element_type=F32))
        o = jnp.concatenate(outs, axis=0)
        ones = jnp.ones((HD, HD), BF16)
        row_in_block = lax.broadcasted_iota(jnp.int32, (C, HD), 0) % SB
        cpad_ref[SB:SB + C, :] = c
        kpad_ref[SB:SB + C, :] = k
        vpad_ref[SB:SB + C, :] = v16.astype(F32)
        for d in range(SB):
            cs = cpad_ref[SB - d:SB - d + C, :]
            ks = kpad_ref[SB - d:SB - d + C, :]
            vs = vpad_ref[SB - d:SB - d + C, :]
            wgt = jnp.where(row_in_block >= d, q * ks * jnp.exp(c - cs), 0.0)
            s = jnp.dot(wgt.astype(BF16), ones, preferred_element_type=F32)
            o = o + s * vs
        return o

    def chunk(n, st, intra, lanes):
        r0 = pl.multiple_of(n * C, C)
        lf = lf_ref[0, pl.ds(r0, C), lanes]
        q = q_ref[0, pl.ds(r0, C), lanes]
        v16 = v_ref[0, pl.ds(r0, C), lanes]
        k = 1.0 - jnp.exp(lf)
        c = None
        rem = lf
        for p in range(3):
            piece = rem.astype(BF16)
            term = jnp.dot(tril, piece, preferred_element_type=F32)
            c = term if c is None else c + term
            rem = rem - piece.astype(F32)
        c_end = c[C - 1:C, :]

        qe = (q * jnp.exp(c)).astype(BF16)
        o = lax.dot_general(qe, st.astype(BF16), (((1,), (1,)), ((), ())),
                            preferred_element_type=F32)
        kd = (k * jnp.exp(c_end - c)).astype(BF16)
        upd = lax.dot_general(v16, kd, (((0,), (0,)), ((), ())),
                              preferred_element_type=F32)
        o = o + intra(q, k, v16, c)
        o_ref[0, pl.ds(r0, C), lanes] = (
            _rms(o, gain) * g_ref[0, pl.ds(r0, C), lanes]).astype(BF16)
        return st * jnp.exp(c_end) + upd

    G = HG_GROUP
    R = G * C
    gr = lax.broadcasted_iota(jnp.int32, (R, R), 0)
    gc = lax.broadcasted_iota(jnp.int32, (R, R), 1)
    tril_group = ((gr // C == gc // C) & (gr >= gc)).astype(BF16)
    causal = r_i >= c_i

    def stage_cumsum(i):
        lf = lf_ref[0, i * R:(i + 1) * R, :]
        c = None
        rem = lf
        for p in range(3):
            piece = rem.astype(BF16)
            term = jnp.dot(tril_group, piece, preferred_element_type=F32)
            c = term if c is None else c + term
            rem = rem - piece.astype(F32)
        return lf, c

    def stage_scores(i, lanes, lf, c):
        lf, c = lf[:, lanes], c[:, lanes]
        q = q_ref[0, i * R:(i + 1) * R, lanes]
        v16 = v_ref[0, i * R:(i + 1) * R, lanes]
        k = 1.0 - jnp.exp(lf)
        lhs, rhs, c_ends = [], [], []
        for g in range(G):
            cg = c[g * C:(g + 1) * C, :]
            qg = q[g * C:(g + 1) * C, :]
            kg = k[g * C:(g + 1) * C, :]
            refs = [jnp.zeros((1, HD), F32)] + [cg[b * SB - 1:b * SB, :] for b in range(1, n_sub)]
            own_ref = jnp.concatenate([jnp.broadcast_to(r, (SB, HD)) for r in refs], axis=0)
            k_own = (kg * jnp.exp(own_ref - cg)).astype(BF16)
            lhs_cols, rhs_cols = [], []
            for b in range(n_sub):
                lo = b * SB
                q_b = (qg[lo:, :] * jnp.exp(cg[lo:, :] - refs[b])).astype(BF16)
                k_b = k_own[lo:lo + SB, :]
                above = [jnp.zeros((lo, HD), BF16)] if lo else []
                below = [jnp.zeros((C - lo - SB, HD), BF16)] if lo + SB < C else []
                lhs_cols.append(jnp.concatenate(above + [q_b], axis=0))
                rhs_cols.append(jnp.concatenate(above + [k_b] + below, axis=0))
            lhs.append(jnp.concatenate(lhs_cols, axis=1))
            rhs.append(jnp.concatenate(rhs_cols, axis=1))
            c_ends.append(cg[C - 1:C, :])
        c_end_rows = jnp.concatenate([jnp.broadcast_to(e, (C, HD)) for e in c_ends], axis=0)
        qe = (q * jnp.exp(c)).astype(BF16)
        kd = (k * jnp.exp(c_end_rows - c)).astype(BF16)

        scores = [lax.dot_general(lhs[g], rhs[g], (((1,), (1,)), ((), ())),
                                  preferred_element_type=F32) for g in range(G)]
        upds = [lax.dot_general(v16[g * C:(g + 1) * C, :], kd[g * C:(g + 1) * C, :],
                                (((0,), (0,)), ((), ())), preferred_element_type=F32)
                for g in range(G)]
        return scores, upds, qe, c_ends

    def stage_output(i, lanes, prepared, st):
        scores, upds, qe, c_ends = prepared
        v16 = v_ref[0, i * R:(i + 1) * R, lanes]
        intra = [jnp.dot(jnp.where(causal, scores[g], 0.0).astype(BF16),
                         v16[g * C:(g + 1) * C, :], preferred_element_type=F32)
                 for g in range(G)]
        outs = []
        for g in range(G):
            inter = lax.dot_general(qe[g * C:(g + 1) * C, :], st.astype(BF16),
                                    (((1,), (1,)), ((), ())), preferred_element_type=F32)
            outs.append(inter + intra[g])
            st = st * jnp.exp(c_ends[g]) + upds[g]
        o = jnp.concatenate(outs, axis=0)
        rows = slice(i * R, (i + 1) * R)
        o_ref[0, rows, lanes] = (_rms(o, gain) * g_ref[0, rows, lanes]).astype(BF16)
        return st

    head_lanes = [slice(hh * HD, (hh + 1) * HD) for hh in range(HG_HEADS_PER_STEP)]

    def run_factored():
        n_groups = SEQ // R
        first = stage_cumsum(0)
        prepared = [stage_scores(0, lanes, *first) for lanes in head_lanes]
        states = [jnp.zeros((HD, HD), F32) for _ in head_lanes]
        for i in range(n_groups):
            if i + 1 < n_groups:
                ahead = stage_cumsum(i + 1)
            states = [stage_output(i, lanes, prepared[hh], states[hh])
                      for hh, lanes in enumerate(head_lanes)]
            if i + 1 < n_groups:
                prepared = [stage_scores(i + 1, lanes, *ahead) for lanes in head_lanes]

    def run_exact():
        for lanes in head_lanes:
            def body(n, st, lanes=lanes):
                return chunk(n, st, intra_exact, lanes)
            lax.fori_loop(0, SEQ // C, body, jnp.zeros((HD, HD), F32))

    blk = lax.broadcasted_iota(jnp.int32, (SEQ // SB, SEQ), 0)
    tok = lax.broadcasted_iota(jnp.int32, (SEQ // SB, SEQ), 1) // SB
    block_decay = jnp.dot((blk == tok).astype(BF16), lf_ref[0].astype(BF16),
                          preferred_element_type=F32)
    mild = jnp.min(block_decay) >= -HG_SAFE_DECAY

    @pl.when(mild)
    def _():
        run_factored()

    @pl.when(jnp.logical_not(mild))
    def _():
        zpad = jnp.zeros((SB, HD), F32)
        cpad_ref[0:SB, :] = zpad
        kpad_ref[0:SB, :] = zpad
        vpad_ref[0:SB, :] = zpad
        run_exact()


def _hgrn(layer, q, lf, v, g, o_gains):
    spec = pl.BlockSpec((1, SEQ, HG_HEADS_PER_STEP * HG_HEAD_DIM), lambda b, h: (b, 0, h))
    return pl.pallas_call(
        _hgrn_kernel,
        grid=(BATCH, HG_HEADS // HG_HEADS_PER_STEP),
        in_specs=[spec, spec, spec, spec, _layer((1, HG_HEAD_DIM), layer)],
        out_specs=spec,
        out_shape=jax.ShapeDtypeStruct((BATCH, SEQ, MAIN_WIDTH), BF16),
        scratch_shapes=[pltpu.VMEM((HG_SUB + HG_CHUNK, HG_HEAD_DIM), F32),
                        pltpu.VMEM((HG_SUB + HG_CHUNK, HG_HEAD_DIM), F32),
                        pltpu.VMEM((HG_SUB + HG_CHUNK, HG_HEAD_DIM), F32)],
        compiler_params=_params(("parallel", "parallel")),
        name="hgrn",
    )(q, lf, v, g, o_gains)


FOX_BIAS_PIECES = 3
FOX_PIECE_STRIDE = 16
LOG2E = 1.4426950408889634


def _bias_slot_layout():
    lane = lax.broadcasted_iota(jnp.int32, (V7X_LANES, MAIN_WIDTH), 1)
    pair = lane // (2 * FOX_HEAD_DIM)
    half = (lane // FOX_HEAD_DIM) % 2
    head_here = 2 * pair + (1 - half)
    slot = lane % FOX_HEAD_DIM
    return head_here, slot


def _place_bias(packed):
    head_here, slot = _bias_slot_layout()
    src = lax.broadcasted_iota(jnp.int32, (V7X_LANES, MAIN_WIDTH), 0)
    src_piece, src_head = src // FOX_PIECE_STRIDE, src % FOX_PIECE_STRIDE
    hit = (src_piece < FOX_BIAS_PIECES) & (src_head == head_here)
    place = jnp.concatenate([(hit & (slot == src_piece)).astype(BF16),
                             (hit & (slot == src_piece + FOX_BIAS_PIECES)).astype(BF16)],
                            axis=1)
    return jnp.dot(packed, place, preferred_element_type=F32)


def _with_bias_lanes(pair_vals, pair_bias, head_in_pair):
    own = (lax.broadcasted_iota(jnp.int32, pair_vals.shape, 1) // FOX_HEAD_DIM) == head_in_pair
    return jnp.where(own, pair_vals, pair_bias)


def _kv_kernel(x_ref, gain_ref, wk_ref, wv_ref, wf_ref, fb_ref, kg_ref,
               ka_ref, vt_ref, qbias_ref, carry_ref, v_ref):
    @pl.when(pl.program_id(1) == 0)
    def _():
        carry_ref[...] = jnp.zeros_like(carry_ref)

    n = _rms(x_ref[0], gain_ref[...]).astype(BF16)

    z = _mm(n, wf_ref[...]) + fb_ref[...]
    v_ref[...] = _mm(n, wv_ref[...])
    k_raw = [_mm(n, wk_ref[:, c * V7X_MXU_COLS:(c + 1) * V7X_MXU_COLS])
             for c in range(MAIN_WIDTH // V7X_MXU_COLS)]
    log_f = jnp.minimum(z, 0.0) - jnp.log1p(jnp.exp(-jnp.abs(z)))
    seg = KV_CUM_SEG
    tril = (lax.broadcasted_iota(jnp.int32, (seg, seg), 0)
            >= lax.broadcasted_iota(jnp.int32, (seg, seg), 1)).astype(BF16)
    local = []
    for sgi in range(KV_TILE // seg):
        rem = log_f[sgi * seg:(sgi + 1) * seg, :]
        acc = None
        for p in range(3):
            piece = rem.astype(BF16)
            term = jnp.dot(tril, piece, preferred_element_type=F32)
            acc = term if acc is None else acc + term
            rem = rem - piece.astype(F32)
        local.append(acc)
    last = carry_ref[...]
    parts = []
    for acc in local:
        parts.append(acc + last)
        last = parts[-1][seg - 1:seg, :]
    carry_ref[...] = last
    cum = jnp.concatenate(parts, axis=0)

    lane = lax.broadcasted_iota(jnp.int32, cum.shape, 1)
    rem = jnp.where(lane < FOX_HEADS, cum * LOG2E, 0.0)
    packed = None
    for p in range(FOX_BIAS_PIECES):
        piece = rem.astype(BF16).astype(F32)
        moved = pltpu.roll(piece, p * FOX_PIECE_STRIDE, axis=1) if p else piece
        packed = moved if packed is None else packed + moved
        rem = rem - piece
    placed = _place_bias(packed.astype(BF16))
    _, slot = _bias_slot_layout()
    slot_row = slot[0:1, :]
    n_p = FOX_BIAS_PIECES
    k_bias = (placed[:, :MAIN_WIDTH]
              + ((slot_row >= n_p) & (slot_row < 2 * n_p)).astype(F32)).astype(BF16)
    qbias_ref[0] = (placed[:, MAIN_WIDTH:] - (slot_row < n_p).astype(F32)).astype(BF16)

    pair_w = 2 * FOX_HEAD_DIM
    for c in range(MAIN_WIDTH // V7X_MXU_COLS):
        sl = slice(c * V7X_MXU_COLS, (c + 1) * V7X_MXU_COLS)
        k_n = _group_rms(k_raw[c], kg_ref[:, sl], FOX_HEAD_DIM).astype(BF16)
        for pp in range(V7X_MXU_COLS // pair_w):
            lo = c * V7X_MXU_COLS + pp * pair_w
            for hh in range(2):
                dst = 2 * lo + hh * pair_w
                ka_ref[0, :, dst:dst + pair_w] = _with_bias_lanes(
                    k_n[:, pp * pair_w:(pp + 1) * pair_w], k_bias[:, lo:lo + pair_w], hh)
    vt_ref[0] = v_ref[...].T.astype(BF16)


def _kv(x, gain, w_kv, wf, f_bias, k_gain_tiled):
    row = lambda w: pl.BlockSpec((1, KV_TILE, w), lambda b, i: (b, i, 0))
    return pl.pallas_call(
        _kv_kernel,
        grid=(BATCH, SEQ // KV_TILE),
        in_specs=[row(D_MODEL), _resident((1, D_MODEL)),
                  _resident((D_MODEL, MAIN_WIDTH), (0, 0)),
                  _resident((D_MODEL, MAIN_WIDTH), (0, 1)), _resident((D_MODEL, V7X_LANES)),
                  _resident((1, V7X_LANES)), _resident((1, MAIN_WIDTH))],
        out_specs=[row(2 * MAIN_WIDTH),
                   pl.BlockSpec((1, MAIN_WIDTH, KV_TILE), lambda b, i: (b, 0, i)),
                   row(MAIN_WIDTH)],
        out_shape=[jax.ShapeDtypeStruct((BATCH, SEQ, 2 * MAIN_WIDTH), BF16),
                   jax.ShapeDtypeStruct((BATCH, MAIN_WIDTH, SEQ), BF16),
                   jax.ShapeDtypeStruct((BATCH, SEQ, MAIN_WIDTH), BF16)],
        scratch_shapes=[pltpu.VMEM((1, V7X_LANES), F32),
                        pltpu.VMEM((KV_TILE, MAIN_WIDTH), F32)],
        compiler_params=_params(("parallel", "arbitrary")),
        name="kv",
    )(x, gain, w_kv, w_kv, wf, f_bias, k_gain_tiled)


def _fox_kernel(qa_ref, ka_ref, vt_ref, gate_ref, o_ref, s_ref, m_ref, acc_ref):
    T, HD = FOX_TILE, FOX_HEAD_DIM
    W = 2 * HD
    H = T // 2
    NH = FOX_HEADS_PER_STEP

    def pieces(i, j):
        if j == i:
            return [(slice(0, H), slice(0, T)), (slice(H, T), slice(H, T))]
        return [(slice(0, T), slice(0, T))]

    def issue_scores(slot, i, j):
        for hh in range(NH):
            for keys, cols in pieces(i, j):
                s_ref[slot, hh, keys, cols] = lax.dot_general(
                    ka_ref[0, j * T + keys.start:j * T + keys.stop, hh * W:(hh + 1) * W],
                    qa_ref[0, i * T + cols.start:i * T + cols.stop, hh * W:(hh + 1) * W],
                    (((1,), (1,)), ((), ())), preferred_element_type=F32)

    def consume(slot, i, j):
        for n_piece, (keys, cols) in enumerate(pieces(i, j)):
            n_k, n_q = keys.stop - keys.start, cols.stop - cols.start
            first = j == 0 and n_piece == 0
            probs, alphas = [], []
            for hh in range(NH):
                s = s_ref[slot, hh, keys, cols]
                if j == i:
                    key_pos = lax.broadcasted_iota(jnp.int32, (n_k, n_q), 0) + keys.start
                    query_pos = lax.broadcasted_iota(jnp.int32, (n_k, n_q), 1) + cols.start
                    s = jnp.where(key_pos <= query_pos, s, -jnp.inf)
                m_new = jnp.max(s, axis=0, keepdims=True)
                if not first:
                    m_old = m_ref[hh, :, cols]
                    m_new = jnp.maximum(m_old, m_new)
                    alphas.append(jnp.exp2(m_old - m_new))
                probs.append(jnp.exp2(s - m_new).astype(BF16))
                m_ref[hh, :, cols] = m_new
            for hh in range(NH):
                vt = jnp.concatenate(
                    [vt_ref[0, hh * HD:(hh + 1) * HD, j * T + keys.start:j * T + keys.stop],
                     jnp.ones((ONES_ROWS, n_k), BF16)], axis=0)
                pv = jnp.dot(vt, probs[hh], preferred_element_type=F32)
                acc_ref[hh, :, cols] = pv if first else alphas[hh] * acc_ref[hh, :, cols] + pv
        if j == i:
            out_t = jnp.concatenate([acc_ref[hh, 0:HD, :] / acc_ref[hh, HD:HD + 1, :]
                                     for hh in range(NH)], axis=0)
            rows = slice(i * T, (i + 1) * T)
            o_ref[0, rows, :] = (out_t.T * gate_ref[0, rows, :]).astype(BF16)

    blocks = [(i, j) for i in range(SEQ // T) for j in range(i + 1)]
    issue_scores(0, *blocks[0])
    for n, (i, j) in enumerate(blocks):
        if n + 1 < len(blocks):
            issue_scores((n + 1) % 2, *blocks[n + 1])
        consume(n % 2, i, j)


def _fox(qa, ka, vt, gate):
    T = FOX_TILE
    W = 2 * FOX_HEAD_DIM
    NH = FOX_HEADS_PER_STEP
    seq_spec = lambda w: pl.BlockSpec((1, SEQ, w), lambda b, p: (b, 0, p))
    return pl.pallas_call(
        _fox_kernel,
        grid=(BATCH, FOX_HEADS // NH),
        in_specs=[seq_spec(NH * W), seq_spec(NH * W),
                  pl.BlockSpec((1, NH * FOX_HEAD_DIM, SEQ), lambda b, p: (b, p, 0)),
                  seq_spec(NH * FOX_HEAD_DIM)],
        out_specs=seq_spec(NH * FOX_HEAD_DIM),
        out_shape=jax.ShapeDtypeStruct((BATCH, SEQ, MAIN_WIDTH), BF16),
        scratch_shapes=[pltpu.VMEM((2, NH, T, T), F32),
                        pltpu.VMEM((NH, 1, T), F32),
                        pltpu.VMEM((NH, FOX_HEAD_DIM + ONES_ROWS, T), F32)],
        compiler_params=_params(("parallel", "parallel")),
        name="fox",
    )(qa, ka, vt, gate)


def _memattn_kernel(qm_ref, mem_ref, mgain_ref, wkv_ref, qg_ref, kg_ref, o_ref, kv_ref):
    W, HD = MEM_WIDTH, MEM_HEAD_DIM
    batch = range(MEM_BATCH)
    heads = range(MEM_HEADS)
    ones_rows = jnp.ones((ONES_ROWS, N_MEM), BF16)
    scale = HD ** -0.5 * LOG2E

    for bi in batch:
        mem_n = _rms(mem_ref[bi], mgain_ref[...]).astype(BF16)
        kv_ref[bi] = _mm(mem_n, wkv_ref[...])
    keys, values_t = [], []
    for bi in batch:
        km = _group_rms(kv_ref[bi, :, :W], kg_ref[...], HD).astype(BF16)
        head_of_lane = lax.broadcasted_iota(jnp.int32, km.shape, 1) // HD
        vm_t = kv_ref[bi, :, W:].T.astype(BF16)
        keys.append([jnp.where(head_of_lane == h, km, jnp.zeros_like(km)) for h in heads])
        values_t.append([jnp.concatenate([vm_t[h * HD:(h + 1) * HD, :], ones_rows], axis=0)
                         for h in heads])
    queries = [(_group_rms(qm_ref[bi], qg_ref[...], HD) * scale).astype(BF16) for bi in batch]
    scores = [[lax.dot_general(keys[bi][h], queries[bi], (((1,), (1,)), ((), ())),
                               preferred_element_type=F32) for h in heads] for bi in batch]
    for bi in batch:
        outs = []
        for h in heads:
            s = scores[bi][h]
            p = jnp.exp2(s - jnp.max(s, axis=0, keepdims=True)).astype(BF16)
            r = jnp.dot(values_t[bi][h], p, preferred_element_type=F32)
            outs.append(r[0:HD, :] / r[HD:HD + 1, :])
        o_ref[bi] = jnp.concatenate(outs, axis=0).T.astype(BF16)


def _memattn(layer, qm, mem, mem_gains, w_mem_kv, q_gains_tiled, k_gains_tiled):
    qspec = pl.BlockSpec((MEM_BATCH, SEQ, MEM_WIDTH), lambda b: (b, 0, 0))
    return pl.pallas_call(
        _memattn_kernel,
        grid=(BATCH // MEM_BATCH,),
        in_specs=[qspec,
                  pl.BlockSpec((MEM_BATCH, N_MEM, D_MODEL), lambda b: (b, 0, 0)),
                  _layer((1, D_MODEL), layer), _layer((D_MODEL, 2 * MEM_WIDTH), layer),
                  _layer((1, MEM_WIDTH), layer), _layer((1, MEM_WIDTH), layer)],
        out_specs=qspec,
        out_shape=jax.ShapeDtypeStruct((BATCH, SEQ, MEM_WIDTH), BF16),
        scratch_shapes=[pltpu.VMEM((MEM_BATCH, N_MEM, 2 * MEM_WIDTH), F32)],
        compiler_params=_params(("parallel",)),
        name="memattn",
    )(qm, mem, mem_gains, w_mem_kv, q_gains_tiled, k_gains_tiled)


def kernel(x, mem, ffn1_norm, ffn1_w_gate, ffn1_w_up, ffn1_w_down, mix_norm, mem_norm,
           w_mem_kv, mem_q_gain, mem_k_gain, w_in_a, hgrn_lb_logits, hgrn_o_gain,
           w_in_b, fox_q_gain, kv_norm, w_kv, fox_f_bias, fox_k_gain, w_out,
           ffn2_norm, ffn2_w_gate, ffn2_w_up, ffn2_w_down):
    row = lambda t: t.reshape(1, -1).astype(F32)
    rows = lambda t, reps=1: jnp.tile(t.astype(F32), (1, reps)).reshape(t.shape[0], 1, -1)
    f32 = lambda t: t.astype(F32)
    x = x.reshape(TOKENS, D_MODEL).astype(F32)
    mem = mem.astype(F32)
    lb_logits = hgrn_lb_logits.astype(F32)
    ffn1 = (rows(ffn1_norm), f32(ffn1_w_gate), f32(ffn1_w_up), f32(ffn1_w_down))
    ffn2 = (rows(ffn2_norm), f32(ffn2_w_gate), f32(ffn2_w_up), f32(ffn2_w_down))
    mix_gains = rows(mix_norm)
    o_gains, fox_q_gains = rows(hgrn_o_gain), rows(fox_q_gain, FOX_HEADS)
    mem_args = (rows(mem_norm), f32(w_mem_kv), rows(mem_q_gain, MEM_HEADS),
                rows(mem_k_gain, MEM_HEADS))

    for l in range(DEPTH):
        x = _ffn(l, x, *ffn1)
        if l < N_A_LAYERS:
            qs, lf, iv, sg, qm = _inproj_a(l, x, mix_gains, f32(w_in_a), lb_logits)
            to3 = lambda t: t.reshape(BATCH, SEQ, MAIN_WIDTH)
            main = _hgrn(l, to3(qs), to3(lf), to3(iv), to3(sg), o_gains)
        else:
            qa, gate, qm = _inproj_b(l, x, mix_gains, f32(w_in_b), fox_q_gains,
                                     q_bias.reshape(TOKENS, MAIN_WIDTH))
            main = _fox(qa.reshape(BATCH, SEQ, 2 * MAIN_WIDTH), k_aug, v_t,
                        gate.reshape(BATCH, SEQ, MAIN_WIDTH))
        memo = _memattn(l, qm.reshape(BATCH, SEQ, MEM_WIDTH), mem, *mem_args)
        x = _mix_ffn(l, x, main.reshape(TOKENS, MAIN_WIDTH), memo.reshape(TOKENS, MEM_WIDTH),
                     f32(w_out), *ffn2)
        if l == N_A_LAYERS - 1:
            pad = V7X_LANES - FOX_HEADS
            wf = jnp.pad(f32(w_kv[:, 2 * MAIN_WIDTH:]), ((0, 0), (0, pad)))
            fb = jnp.pad(fox_f_bias.astype(F32), (0, pad)).reshape(1, -1)
            k_aug, v_t, q_bias = _kv(x.reshape(BATCH, SEQ, D_MODEL), row(kv_norm),
                                     f32(w_kv), wf, fb,
                                     row(jnp.tile(fox_k_gain, FOX_HEADS)))
    return x.reshape(BATCH, SEQ, D_MODEL)
```
